```python
import jax
import jax.numpy as jnp
from jax import lax
import numpy as np

D_MODEL = 1024
BATCH = 2
SEQ = 16384
DEPTH = 4

GRID_W = 64
CTX_LEN = 256
EPS = 1e-6

LRU_WIDTH = D_MODEL
LRU_HEADS = 8
LRU_HEAD_DIM = LRU_WIDTH // LRU_HEADS
LRU_C = 8.0
CONV_W = 4
CONV_LEFT = 2
FNET_WIDTH = D_MODEL
FNET_GROUPS = 4
FNET_GROUP_DIM = FNET_WIDTH // FNET_GROUPS
EVEN_IN = 2 * (LRU_WIDTH + FNET_WIDTH)
EVEN_MIX = LRU_WIDTH + FNET_WIDTH

RET_HEADS = 4
RET_QK_DIM = D_MODEL // RET_HEADS
RET_V_DIM = 2 * D_MODEL // RET_HEADS
RET_QK = RET_HEADS * RET_QK_DIM
RET_MIX = RET_HEADS * RET_V_DIM
ODD_IN = 2 * RET_QK + 2 * RET_MIX
CHUNK = 128
ROPE_THETA = 10000.0
ROPE_FREQS = RET_QK_DIM // 4

kernel_name = 'hybrid_rglru_fourier_retention_prefix_trunk'


def rms_norm(x, g):
    xf = x.astype(jnp.float32)
    y = xf * lax.rsqrt(jnp.mean(xf * xf, axis=-1, keepdims=True) + EPS)
    return (y * g.astype(jnp.float32)).astype(x.dtype)


def ada_mod(cond, w, b):
    m = jax.nn.silu(cond) @ w + b
    return jnp.split(m, 3, axis=-1)


def dwconv(z, w, b):
    T = z.shape[1]
    zp = jnp.pad(z, ((0, 0), (CONV_LEFT, CONV_W - 1 - CONV_LEFT), (0, 0)))
    out = b + zp[:, 0:T] * w[0]
    for tap in range(1, CONV_W):
        out = out + zp[:, tap:tap + T] * w[tap]
    return out


def lru_coeffs(z, wa, ba, wx, bx, lam):
    B_, T, W = z.shape
    zf = z.astype(jnp.float32)
    zh = zf.reshape(B_, T, LRU_HEADS, LRU_HEAD_DIM)
    r = jax.nn.sigmoid(jnp.einsum('bthi,hij->bthj', zh, wa.astype(jnp.float32)).reshape(B_, T, W) + ba.astype(jnp.float32))
    i = jax.nn.sigmoid(jnp.einsum('bthi,hij->bthj', zh, wx.astype(jnp.float32)).reshape(B_, T, W) + bx.astype(jnp.float32))
    log_a = -LRU_C * r * jax.nn.softplus(-lam.astype(jnp.float32))
    a = jnp.exp(log_a)
    bterm = jnp.sqrt(-jnp.expm1(2.0 * log_a)) * (i * zf)
    return a, bterm


def _lin_combine(e1, e2):
    a1, b1 = e1
    a2, b2 = e2
    return a1 * a2, a2 * b1 + b2


def linear_scan(a, bterm, h0, reverse):
    A, H = lax.associative_scan(_lin_combine, (a, bterm), axis=1, reverse=reverse)
    if h0 is None:
        return H
    return H + A * h0[:, None, :]


def rglru_branch(xl, xc, conv_w, conv_b, wa, ba, wx, bx, lam):
    zl = dwconv(xl, conv_w, conv_b)
    zc = dwconv(xc, conv_w, conv_b)
    yl = None
    yc = None
    for d in range(2):
        rev = d == 1
        ac, bc = lru_coeffs(zc, wa[d], ba[d], wx[d], bx[d], lam[d])
        hc = linear_scan(ac, bc, None, rev)
        h_end = hc[:, 0] if rev else hc[:, -1]
        al, bl = lru_coeffs(zl, wa[d], ba[d], wx[d], bx[d], lam[d])
        hl = linear_scan(al, bl, h_end, rev)
        yl = hl if yl is None else yl + hl
        yc = hc if yc is None else yc + hc
    return yl.astype(xl.dtype), yc.astype(xc.dtype)


def fourier_branch(z, w_f):
    B_, T, W = z.shape
    zg = z.astype(jnp.float32).reshape(B_, T, FNET_GROUPS, FNET_GROUP_DIM)
    f = jnp.real(jnp.fft.fft2(zg, axes=(1, 3), norm='ortho'))
    y = jnp.einsum('btgi,gij->btgj', f, w_f.astype(jnp.float32))
    return y.reshape(B_, T, W).astype(z.dtype)


def even_mixer(h, hc, w_in, w_out, conv_w, conv_b, wa, ba, wx, bx, lam, w_f, need_ctx):
    cuts = [LRU_WIDTH, LRU_WIDTH + FNET_WIDTH, 2 * LRU_WIDTH + FNET_WIDTH]
    xa, xb, ga, gb = jnp.split(h @ w_in, cuts, axis=-1)
    xca, xcb, gca, gcb = jnp.split(hc @ w_in, cuts, axis=-1)
    ya, yca = rglru_branch(xa, xca, conv_w, conv_b, wa, ba, wx, bx, lam)
    yb = fourier_branch(xb, w_f)
    y = jnp.concatenate([ya * jax.nn.silu(ga), yb * jax.nn.silu(gb)], axis=-1) @ w_out
    if not need_ctx:
        return y, None
    ycb = fourier_branch(xcb, w_f)
    yc = jnp.concatenate([yca * jax.nn.silu(gca), ycb * jax.nn.silu(gcb)], axis=-1) @ w_out
    return y, yc


def axial_rope_tables(rows):
    row = jnp.repeat(jnp.arange(rows, dtype=jnp.float32), GRID_W)
    col = jnp.tile(jnp.arange(GRID_W, dtype=jnp.float32), rows)
    inv = ROPE_THETA ** (-jnp.arange(ROPE_FREQS, dtype=jnp.float32) / ROPE_FREQS)
    ang = jnp.stack([row[:, None] * inv, col[:, None] * inv], axis=1)
    return jnp.cos(ang), jnp.sin(ang)


def apply_axial_rope(t, cos, sin):
    B_, T, H, _ = t.shape
    t = t.reshape(B_, T, H, 2, 2, ROPE_FREQS)
    t1 = t[..., 0, :]
    t2 = t[..., 1, :]
    cs = cos[None, :, None]
    sn = sin[None, :, None]
    out = jnp.stack([t1 * cs - t2 * sn, t2 * cs + t1 * sn], axis=-2)
    return out.reshape(B_, T, H, RET_QK_DIM)


def retention_chunks(q, k, v, log_g, s0, reverse):
    B_, T, H, dk = q.shape
    dv = v.shape[-1]
    n = T // CHUNK

    def to_chunks(t):
        return t.reshape(B_, n, CHUNK, H, t.shape[-1]).transpose(1, 0, 3, 2, 4)

    lg = log_g.astype(jnp.float32)
    idx = jnp.arange(CHUNK, dtype=jnp.float32)
    diff = idx[None, :] - idx[:, None]
    if reverse:
        mask = diff > 0
        expo = diff
        q_exp = CHUNK - idx
        k_exp = idx
    else:
        mask = diff <= 0
        expo = -diff
        q_exp = idx + 1.0
        k_exp = (CHUNK - 1.0) - idx
    dmat = jnp.where(mask, jnp.exp(lg[:, None, None] * jnp.where(mask, expo, 0.0)), 0.0)
    q_dec = jnp.exp(lg[:, None] * q_exp)[:, :, None]
    k_dec = jnp.exp(lg[:, None] * k_exp)[:, :, None]
    c_dec = jnp.exp(lg * CHUNK)[:, None, None]

    def step(s, qkv):
        qn, kn, vn = qkv
        sc = jnp.einsum('bhid,bhjd->bhij', qn, kn) * dmat
        o = jnp.einsum('bhij,bhjv->bhiv', sc, vn) + jnp.einsum('bhid,bhdv->bhiv', qn * q_dec, s)
        s_new = c_dec * s + jnp.einsum('bhjd,bhjv->bhdv', kn * k_dec, vn)
        return s_new, o

    if s0 is None:
        s0 = jnp.zeros((B_, H, dk, dv), jnp.float32)
    s_end, o = lax.scan(step, s0, (to_chunks(q), to_chunks(k), to_chunks(v)), reverse=reverse)
    o = o.transpose(1, 0, 3, 2, 4).reshape(B_, T, H, dv)
    return o, s_end


def retention_state(k, v, log_g, reverse):
    T = k.shape[1]
    pos = jnp.arange(T, dtype=jnp.float32)
    expo = pos if reverse else (T - 1.0) - pos
    w = jnp.exp(log_g.astype(jnp.float32)[:, None] * expo[None, :])
    return jnp.einsum('bthd,ht,bthv->bhdv', k, w, v)


def head_norm(o):
    mu = jnp.mean(o, axis=-1, keepdims=True)
    var = jnp.mean(jnp.square(o - mu), axis=-1, keepdims=True)
    return (o - mu) * lax.rsqrt(var + EPS)


def retention_mixer(h, hc, w_in, w_out, log_gamma, cos, sin, need_ctx):
    def proj(z):
        B_, T, _ = z.shape
        u = (z @ w_in).astype(jnp.float32)
        q, k, v, g = jnp.split(u, [RET_QK, 2 * RET_QK, 2 * RET_QK + RET_MIX], axis=-1)
        q = q.reshape(B_, T, RET_HEADS, RET_QK_DIM)
        k = k.reshape(B_, T, RET_HEADS, RET_QK_DIM) * (RET_QK_DIM ** -0.5)
        v = v.reshape(B_, T, RET_HEADS, RET_V_DIM)
        return q, k, v, g

    q, k, v, g = proj(h)
    q = apply_axial_rope(q, cos, sin)
    k = apply_axial_rope(k, cos, sin)
    qc, kc, vc, gc = proj(hc)
    o = None
    oc = None
    for d in range(2):
        rev = d == 1
        lg = log_gamma[d]
        if need_ctx:
            ocd, s_ctx = retention_chunks(qc, kc, vc, lg, None, rev)
            oc = ocd if oc is None else oc + ocd
        else:
            s_ctx = retention_state(kc, vc, lg, rev)
        od, _ = retention_chunks(q, k, v, lg, s_ctx, rev)
        o = od if o is None else o + od
    B_, T = h.shape[0], h.shape[1]
    y = (head_norm(o).reshape(B_, T, RET_MIX) * jax.nn.silu(g)).astype(h.dtype) @ w_out
    if not need_ctx:
        return y, None
    Tc = hc.shape[1]
    yc = (head_norm(oc).reshape(B_, Tc, RET_MIX) * jax.nn.silu(gc)).astype(hc.dtype) @ w_out
    return y, yc


def setup_inputs(seed: int = 0):
    key = jax.random.key(seed)
    ks = jax.random.split(key, 21)
    n_even = (DEPTH + 1) // 2
    n_odd = DEPTH // 2
    f32 = jnp.float32

    def nrm(k, shape, scale):
        return scale * jax.random.normal(k, shape, f32)

    u = jax.random.uniform(ks[16], (n_even, 2, LRU_WIDTH), f32, 0.9, 0.999)
    a = u ** (1.0 / LRU_C)
    base_lg = jnp.log1p(-jnp.exp2(-5.0 - jnp.arange(RET_HEADS, dtype=f32)))
    return {
        'x': nrm(ks[0], (BATCH, SEQ, D_MODEL), 1.0),
        'c': nrm(ks[1], (BATCH, D_MODEL), 1.0),
        'ctx': nrm(ks[2], (BATCH, CTX_LEN, D_MODEL), 1.0),
        'c_ctx': nrm(ks[3], (D_MODEL,), 1.0),
        'mod_w': nrm(ks[4], (DEPTH, D_MODEL, 3 * D_MODEL), D_MODEL ** -0.5),
        'mod_b': nrm(ks[5], (DEPTH, 3 * D_MODEL), 0.01),
        'pre_g': 1.0 + nrm(ks[6], (DEPTH, D_MODEL), 0.02),
        'post_g': 1.0 + nrm(ks[7], (DEPTH, D_MODEL), 0.02),
        'mix_w_in': nrm(ks[8], (n_even, D_MODEL, EVEN_IN), D_MODEL ** -0.5),
        'mix_w_out': nrm(ks[9], (n_even, EVEN_MIX, D_MODEL), EVEN_MIX ** -0.5),
        'conv_w': nrm(ks[10], (n_even, CONV_W, LRU_WIDTH), CONV_W ** -0.5),
        'conv_b': nrm(ks[11], (n_even, LRU_WIDTH), 0.01),
        'lru_wa': nrm(ks[12], (n_even, 2, LRU_HEADS, LRU_HEAD_DIM, LRU_HEAD_DIM), LRU_HEAD_DIM ** -0.5),
        'lru_ba': nrm(ks[13], (n_even, 2, LRU_WIDTH), 0.01),
        'lru_wx': nrm(ks[14], (n_even, 2, LRU_HEADS, LRU_HEAD_DIM, LRU_HEAD_DIM), LRU_HEAD_DIM ** -0.5),
        'lru_bx': nrm(ks[15], (n_even, 2, LRU_WIDTH), 0.01),
        'lru_lam': jnp.log(a) - jnp.log1p(-a),
        'fnet_w': nrm(ks[17], (n_even, FNET_GROUPS, FNET_GROUP_DIM, FNET_GROUP_DIM), FNET_GROUP_DIM ** -0.5),
        'ret_w_in': nrm(ks[18], (n_odd, D_MODEL, ODD_IN), D_MODEL ** -0.5),
        'ret_w_out': nrm(ks[19], (n_odd, RET_MIX, D_MODEL), RET_MIX ** -0.5),
        'ret_log_gamma': base_lg * jnp.exp(nrm(ks[20], (n_odd, 2, RET_HEADS), 0.05)),
    }


def reference(x, c, ctx, c_ctx, mod_w, mod_b, pre_g, post_g, mix_w_in, mix_w_out, conv_w, conv_b,
              lru_wa, lru_ba, lru_wx, lru_bx, lru_lam, fnet_w, ret_w_in, ret_w_out, ret_log_gamma):
    ROWS = x.shape[1] // GRID_W
    cos, sin = axial_rope_tables(ROWS)
    xc = ctx
    for layer in range(DEPTH):
        need_ctx = layer < DEPTH - 1
        shift, scale, gate = ada_mod(c, mod_w[layer], mod_b[layer])
        shift_c, scale_c, gate_c = ada_mod(c_ctx, mod_w[layer], mod_b[layer])
        h = rms_norm(x, pre_g[layer]) * (1.0 + scale[:, None]) + shift[:, None]
        hc = rms_norm(xc, pre_g[layer]) * (1.0 + scale_c) + shift_c
        if layer % 2 == 0:
            e = layer // 2
            y, yc = even_mixer(h, hc, mix_w_in[e], mix_w_out[e], conv_w[e], conv_b[e],
                               lru_wa[e], lru_ba[e], lru_wx[e], lru_bx[e], lru_lam[e], fnet_w[e], need_ctx)
        else:
            j = layer // 2
            y, yc = retention_mixer(h, hc, ret_w_in[j], ret_w_out[j], ret_log_gamma[j], cos, sin, need_ctx)
        x = x + gate[:, None] * rms_norm(y, post_g[layer])
        if need_ctx:
            xc = xc + gate_c * rms_norm(yc, post_g[layer])
    return x
```

```python
import functools
import math

import numpy as np
import jax
import jax.numpy as jnp
from jax import lax
from jax.experimental import pallas as pl
from jax.experimental.pallas import tpu as pltpu

F32 = jnp.float32
BF16 = jnp.bfloat16

EPS = 1e-6
LRU_C = 8.0
LRU_HEADS = 8
FNET_GROUPS = 4
RET_HEADS = 4
GRID_W = 64
ROPE_THETA = 10000.0

LANES = 128
VMEM_LIMIT = 56 * 1024 * 1024

ROW_TILE = 512
LRU_CHUNK = 256
RET_CHUNK = 256
FFT_N2 = 128
FFT_J = 8


def _cparams(n_axes):
    return pltpu.CompilerParams(dimension_semantics=("arbitrary",) * n_axes,
                                vmem_limit_bytes=VMEM_LIMIT)


def _silu(v):
    return v * jax.nn.sigmoid(v)


def _ada_kernel(cond_ref, w_ref, b_ref, o_ref):
    s = _silu(cond_ref[...])
    o_ref[...] = jnp.dot(s, w_ref[...], preferred_element_type=F32,
                         precision=lax.Precision.HIGHEST) + b_ref[...]


def _ada_mod(cond8, mod_w, mod_b):
    depth, d, n3 = mod_w.shape
    tn = d
    return pl.pallas_call(
        _ada_kernel,
        out_shape=jax.ShapeDtypeStruct((depth, 8, n3), F32),
        grid=(depth, n3 // tn),
        in_specs=[
            pl.BlockSpec((8, d), lambda l, j: (0, 0)),
            pl.BlockSpec((None, d, tn), lambda l, j: (l, 0, j)),
            pl.BlockSpec((None, 1, tn), lambda l, j: (l, 0, j)),
        ],
        out_specs=pl.BlockSpec((None, 8, tn), lambda l, j: (l, 0, j)),
        compiler_params=_cparams(2),
        name="ada_mod",
    )(cond8, mod_w, mod_b.reshape(depth, 1, n3))


def _inproj_kernel(x_ref, g_ref, sc_ref, sh_ref, w_ref, *rest, rope_cols, chunk):
    if rope_cols:
        cos_ref, sin_ref, o_ref = rest
    else:
        (o_ref,) = rest
    x = x_ref[...]
    ms = jnp.mean(x * x, axis=-1, keepdims=True)
    h = (x * lax.rsqrt(ms + EPS)) * g_ref[...]
    h = h * (1.0 + sc_ref[...]) + sh_ref[...]
    hb = h.astype(BF16)
    n_out = o_ref.shape[-1]
    for c0 in range(0, n_out, chunk):
        r = jnp.dot(hb, w_ref[:, c0:c0 + chunk], preferred_element_type=F32)
        if c0 < rope_cols:
            parts = []
            for s0 in range(0, chunk, LANES):
                col = (c0 + s0) % (2 * LANES)
                ra = r[:, s0:s0 + LANES]
                parts.append(ra * cos_ref[:, col:col + LANES]
                             + pltpu.roll(ra, LANES // 2, axis=1) * sin_ref[:, col:col + LANES])
            r = jnp.concatenate(parts, axis=1)
        o_ref[:, c0:c0 + chunk] = r.astype(o_ref.dtype)


def _inproj(x, g, scale, shift, w, rope=None):
    b, t, d = x.shape
    n = w.shape[1]
    tm = min(ROW_TILE, t)
    in_specs = [
        pl.BlockSpec((None, tm, d), lambda bi, i: (bi, i, 0)),
        pl.BlockSpec((1, d), lambda bi, i: (0, 0)),
        pl.BlockSpec((None, 1, d), lambda bi, i: (bi, 0, 0)),
        pl.BlockSpec((None, 1, d), lambda bi, i: (bi, 0, 0)),
        pl.BlockSpec((d, n), lambda bi, i: (0, 0), pipeline_mode=pl.Buffered(1)),
    ]
    args = [x, g, scale, shift, w]
    rope_cols = 0
    if rope is not None:
        cos_t, sin_t, rope_cols = rope
        in_specs += [pl.BlockSpec((tm, 2 * LANES), lambda bi, i: (i, 0)),
                     pl.BlockSpec((tm, 2 * LANES), lambda bi, i: (i, 0))]
        args += [cos_t, sin_t]
    return pl.pallas_call(
        functools.partial(_inproj_kernel, rope_cols=rope_cols, chunk=2 * LANES),
        out_shape=jax.ShapeDtypeStruct((b, t, n), BF16),
        grid=(b, t // tm),
        in_specs=in_specs,
        out_specs=pl.BlockSpec((None, tm, n), lambda bi, i: (bi, i, 0)),
        compiler_params=_cparams(2),
        name="inproj",
    )(*args)


def _scan_chunk(a, b, carry, reverse):
    rows = a.shape[0]
    sub = lax.broadcasted_iota(jnp.int32, a.shape, 0) & 7
    for dd in (1, 2, 4):
        if reverse:
            m = sub <= 7 - dd
            sh = rows - dd
        else:
            m = sub >= dd
            sh = dd
        a_sh = pltpu.roll(a, sh, axis=0)
        b_sh = pltpu.roll(b, sh, axis=0)
        b = b + a * jnp.where(m, b_sh, 0.0)
        a = a * jnp.where(m, a_sh, 1.0)
    ng = rows // 8
    outs = [None] * ng
    order = range(ng - 1, -1, -1) if reverse else range(ng)
    for gi in order:
        hg = b[8 * gi:8 * gi + 8] + a[8 * gi:8 * gi + 8] * carry
        carry = hg[0:1] if reverse else hg[7:8]
        outs[gi] = hg
    return jnp.concatenate(outs, axis=0), carry


def _lru_kernel(xa_ref, ga_ref, xca_ref, gca_ref, cw_ref, cb_ref, wg_ref, bg_ref, lam_ref,
                o_ref, oc_ref, xs_ref, hf_ref):
    cw = cw_ref[...]
    cb = cb_ref[...]
    nsp = []
    for d in (0, 1):
        y = -lam_ref[d]
        nsp.append(-LRU_C * (jnp.maximum(y, 0.0) + jnp.log1p(jnp.exp(-jnp.abs(y)))))

    def run_seq(x_ref, g_ref, out_ref, carry_f, carry_r):
        tn = x_ref.shape[0]
        rn = min(LRU_CHUNK, tn)
        nch = tn // rn
        zeros8 = jnp.zeros((8, LANES), F32)
        xs_ref[0:8, :] = zeros8
        xs_ref[8 + tn:16 + tn, :] = zeros8

        def fill(c, carry):
            r0 = pl.multiple_of(c * rn, rn)
            xs_ref[pl.ds(8 + r0, rn), :] = x_ref[pl.ds(r0, rn), :].astype(F32)
            return carry

        lax.fori_loop(0, nch, fill, 0)

        def coeffs(r0, d):
            z = cb + cw[0:1] * xs_ref[pl.ds(r0 + 6, rn), :]
            z = z + cw[1:2] * xs_ref[pl.ds(r0 + 7, rn), :]
            z = z + cw[2:3] * xs_ref[pl.ds(r0 + 8, rn), :]
            z = z + cw[3:4] * xs_ref[pl.ds(r0 + 9, rn), :]
            pre = jnp.dot(z.astype(BF16), wg_ref[d], preferred_element_type=F32) + bg_ref[d]
            r = jax.nn.sigmoid(pre[:, :LANES])
            i = jax.nn.sigmoid(pre[:, LANES:])
            a = jnp.exp(nsp[d] * r)
            return a, jnp.sqrt(1.0 - a * a) * (i * z)

        def fwd_body(c, carry):
            r0 = pl.multiple_of(c * rn, rn)
            a, b = coeffs(r0, 0)
            h, carry = _scan_chunk(a, b, carry, False)
            hf_ref[pl.ds(r0, rn), :] = h
            return carry

        carry_f = lax.fori_loop(0, nch, fwd_body, carry_f)

        def rev_body(cc, carry):
            r0 = pl.multiple_of((nch - 1 - cc) * rn, rn)
            a, b = coeffs(r0, 1)
            h, carry = _scan_chunk(a, b, carry, True)
            g = g_ref[pl.ds(r0, rn), :].astype(F32)
            out_ref[pl.ds(r0, rn), :] = ((hf_ref[pl.ds(r0, rn), :] + h) * _silu(g)).astype(out_ref.dtype)
            return carry

        carry_r = lax.fori_loop(0, nch, rev_body, carry_r)
        return carry_f, carry_r

    zero = jnp.zeros((1, LANES), F32)
    cf, cr = run_seq(xca_ref, gca_ref, oc_ref, zero, zero)
    run_seq(xa_ref, ga_ref, o_ref, cf, cr)


def _lru_branch(u, uc, conv_w, conv_b, wg, bg, lam):
    b, t, _ = u.shape
    tc = uc.shape[1]
    w = conv_w.shape[1]
    nh = w // LANES
    ga0 = 2 * nh
    return pl.pallas_call(
        _lru_kernel,
        out_shape=(jax.ShapeDtypeStruct((b, t, w), BF16), jax.ShapeDtypeStruct((b, tc, w), BF16)),
        grid=(b, nh),
        in_specs=[
            pl.BlockSpec((None, t, LANES), lambda bi, h: (bi, 0, h)),
            pl.BlockSpec((None, t, LANES), lambda bi, h: (bi, 0, ga0 + h)),
            pl.BlockSpec((None, tc, LANES), lambda bi, h: (bi, 0, h)),
            pl.BlockSpec((None, tc, LANES), lambda bi, h: (bi, 0, ga0 + h)),
            pl.BlockSpec((conv_w.shape[0], LANES), lambda bi, h: (0, h)),
            pl.BlockSpec((1, LANES), lambda bi, h: (0, h)),
            pl.BlockSpec((2, None, LANES, 2 * LANES), lambda bi, h: (0, h, 0, 0)),
            pl.BlockSpec((2, None, 1, 2 * LANES), lambda bi, h: (0, h, 0, 0)),
            pl.BlockSpec((2, None, 1, LANES), lambda bi, h: (0, h, 0, 0)),
        ],
        out_specs=(pl.BlockSpec((None, t, LANES), lambda bi, h: (bi, 0, h)),
                   pl.BlockSpec((None, tc, LANES), lambda bi, h: (bi, 0, h))),
        scratch_shapes=[pltpu.VMEM((t + 16, LANES), F32), pltpu.VMEM((t, LANES), F32)],
        compiler_params=_cparams(2),
        name="rglru",
    )(u, u, uc, uc, conv_w, conv_b, wg, bg, lam)


def _w12_kernel(cs_ref, wf_ref, o_ref):
    gd = wf_ref.shape[0]
    wf = wf_ref[...]
    o_ref[:, :gd] = jnp.dot(cs_ref[:gd, :], wf, preferred_element_type=F32,
                            precision=lax.Precision.HIGHEST).astype(o_ref.dtype)
    o_ref[:, gd:] = jnp.dot(cs_ref[gd:, :], wf, preferred_element_type=F32,
                            precision=lax.Precision.HIGHEST).astype(o_ref.dtype)


def _fourier_weights(fnet_w):
    ne, ng, gd, _ = fnet_w.shape
    idx = np.arange(gd)
    ang = 2.0 * np.pi * ((idx[:, None] * idx[None, :]) % gd) / gd
    cs = np.concatenate([np.cos(ang), np.sin(ang)], axis=0) / math.sqrt(gd)
    return pl.pallas_call(
        _w12_kernel,
        out_shape=jax.ShapeDtypeStruct((ne, ng, gd, 2 * gd), BF16),
        grid=(ne, ng),
        in_specs=[pl.BlockSpec((2 * gd, gd), lambda e, g: (0, 0)),
                  pl.BlockSpec((None, None, gd, gd), lambda e, g: (e, g, 0, 0))],
        out_specs=pl.BlockSpec((None, None, gd, 2 * gd), lambda e, g: (e, g, 0, 0)),
        compiler_params=_cparams(2),
        name="fourier_weights",
    )(jnp.asarray(cs, F32), fnet_w)


def _chanmix_kernel(xb_ref, w_ref, ur_ref, ui_ref):
    ng, gd, _ = w_ref.shape
    for g in range(ng):
        r = jnp.dot(xb_ref[:, g * gd:(g + 1) * gd], w_ref[g], preferred_element_type=F32)
        ur_ref[:, g * gd:(g + 1) * gd] = r[:, :gd]
        ui_ref[:, g * gd:(g + 1) * gd] = r[:, gd:]


def _chanmix(u, w12):
    b, t, _ = u.shape
    ng, gd, _ = w12.shape
    w = ng * gd
    tm = min(ROW_TILE, t)
    out = jax.ShapeDtypeStruct((b, t, w), F32)
    return pl.pallas_call(
        _chanmix_kernel,
        out_shape=(out, out),
        grid=(b, t // tm),
        in_specs=[pl.BlockSpec((None, tm, w), lambda bi, i: (bi, i, 1)),
                  pl.BlockSpec((ng, gd, 2 * gd), lambda bi, i: (0, 0, 0))],
        out_specs=(pl.BlockSpec((None, tm, w), lambda bi, i: (bi, i, 0)),
                   pl.BlockSpec((None, tm, w), lambda bi, i: (bi, i, 0))),
        compiler_params=_cparams(2),
        name="fourier_chanmix",
    )(u, w12)


def _dft_a_kernel(ur_ref, ui_ref, ea_ref, twr_ref, twi_ref, br_ref, bi_ref):
    n2 = ur_ref.shape[0]
    nj = ur_ref.shape[1]
    xr = jnp.concatenate([ur_ref[:, j, :] for j in range(nj)], axis=1)
    xi = jnp.concatenate([ui_ref[:, j, :] for j in range(nj)], axis=1)
    rhs = jnp.concatenate([xr, xi], axis=0).astype(BF16)
    res = jnp.dot(ea_ref[...], rhs, preferred_element_type=F32)
    for j in range(nj):
        pr = res[:n2, j * LANES:(j + 1) * LANES]
        pi = res[n2:, j * LANES:(j + 1) * LANES]
        tr = twr_ref[j]
        ti = twi_ref[j]
        br_ref[j] = pr * tr - pi * ti
        bi_ref[j] = pr * ti + pi * tr


def _dft_b_kernel(br_ref, bi_ref, eb_ref, o_ref):
    nj = br_ref.shape[1]
    xr = jnp.concatenate([br_ref[:, j, :] for j in range(nj)], axis=1)
    xi = jnp.concatenate([bi_ref[:, j, :] for j in range(nj)], axis=1)
    rhs = jnp.concatenate([xr, xi], axis=0).astype(BF16)
    res = jnp.dot(eb_ref[...], rhs, preferred_element_type=F32)
    for j in range(nj):
        o_ref[:, j, :] = res[:, j * LANES:(j + 1) * LANES]


def _dft_consts(t):
    n2 = FFT_N2
    n1 = t // n2
    i2 = np.arange(n2)
    a2 = 2.0 * np.pi * ((i2[:, None] * i2[None, :]) % n2) / n2
    er, ei = np.cos(a2), np.sin(a2)
    ea = np.block([[er, -ei], [ei, er]])
    i1 = np.arange(n1)
    atw = 2.0 * np.pi * ((i1[:, None] * i2[None, :]) % t) / t
    a1 = 2.0 * np.pi * ((i1[:, None] * i1[None, :]) % n1) / n1
    eb = np.concatenate([np.cos(a1), -np.sin(a1)], axis=1) / math.sqrt(t)
    return (jnp.asarray(ea, F32).astype(BF16), jnp.asarray(np.cos(atw), F32),
            jnp.asarray(np.sin(atw), F32), jnp.asarray(eb, F32).astype(BF16))


def _seq_dft(ur, ui):
    b, t, w = ur.shape
    n2 = FFT_N2
    n1 = t // n2
    nj = FFT_J
    ea, twr, twi, eb = _dft_consts(t)
    twr = jnp.broadcast_to(twr[:, :, None], (n1, n2, LANES))
    twi = jnp.broadcast_to(twi[:, :, None], (n1, n2, LANES))
    ur4 = ur.reshape(b, n2, n1, w)
    ui4 = ui.reshape(b, n2, n1, w)
    mid = jax.ShapeDtypeStruct((b, n1, n2, w), F32)
    br, bi = pl.pallas_call(
        _dft_a_kernel,
        out_shape=(mid, mid),
        grid=(n1 // nj, b, w // LANES),
        in_specs=[
            pl.BlockSpec((None, n2, nj, LANES), lambda tb, bi_, cb: (bi_, 0, tb, cb)),
            pl.BlockSpec((None, n2, nj, LANES), lambda tb, bi_, cb: (bi_, 0, tb, cb)),
            pl.BlockSpec((2 * n2, 2 * n2), lambda tb, bi_, cb: (0, 0)),
            pl.BlockSpec((nj, n2, LANES), lambda tb, bi_, cb: (tb, 0, 0)),
            pl.BlockSpec((nj, n2, LANES), lambda tb, bi_, cb: (tb, 0, 0)),
        ],
        out_specs=(pl.BlockSpec((None, nj, n2, LANES), lambda tb, bi_, cb: (bi_, tb, 0, cb)),
                   pl.BlockSpec((None, nj, n2, LANES), lambda tb, bi_, cb: (bi_, tb, 0, cb))),
        compiler_params=_cparams(3),
        name="dft_stage_a",
    )(ur4, ui4, ea, twr, twi)
    out = pl.pallas_call(
        _dft_b_kernel,
        out_shape=jax.ShapeDtypeStruct((b, n1, n2, w), F32),
        grid=(b, n2 // nj, w // LANES),
        in_specs=[
            pl.BlockSpec((None, n1, nj, LANES), lambda bi_, kb, cb: (bi_, 0, kb, cb)),
            pl.BlockSpec((None, n1, nj, LANES), lambda bi_, kb, cb: (bi_, 0, kb, cb)),
            pl.BlockSpec((n1, 2 * n1), lambda bi_, kb, cb: (0, 0)),
        ],
        out_specs=pl.BlockSpec((None, n1, nj, LANES), lambda bi_, kb, cb: (bi_, 0, kb, cb)),
        compiler_params=_cparams(3),
        name="dft_stage_b",
    )(br, bi, eb)
    return out.reshape(b, t, w)


def _dft_dense_kernel(ur_ref, ui_ref, ec_ref, o_ref):
    rhs = jnp.concatenate([ur_ref[...], ui_ref[...]], axis=0).astype(BF16)
    o_ref[...] = jnp.dot(ec_ref[...], rhs, preferred_element_type=F32)


def _seq_dft_dense(ur, ui):
    b, t, w = ur.shape
    it = np.arange(t)
    ang = 2.0 * np.pi * ((it[:, None] * it[None, :]) % t) / t
    ec = jnp.asarray(np.concatenate([np.cos(ang), -np.sin(ang)], axis=1) / math.sqrt(t), F32).astype(BF16)
    return pl.pallas_call(
        _dft_dense_kernel,
        out_shape=jax.ShapeDtypeStruct((b, t, w), F32),
        grid=(b,),
        in_specs=[pl.BlockSpec((None, t, w), lambda bi: (bi, 0, 0)),
                  pl.BlockSpec((None, t, w), lambda bi: (bi, 0, 0)),
                  pl.BlockSpec((t, 2 * t), lambda bi: (0, 0))],
        out_specs=pl.BlockSpec((None, t, w), lambda bi: (bi, 0, 0)),
        compiler_params=_cparams(1),
        name="dft_dense",
    )(ur, ui, ec)


def _post(y, x_ref, pg_ref, gate_ref, o_ref):
    ms = jnp.mean(y * y, axis=-1, keepdims=True)
    yn = (y * lax.rsqrt(ms + EPS)) * pg_ref[...]
    o_ref[...] = x_ref[...] + gate_ref[...] * yn


def _outproj_even_kernel(ma_ref, fy_ref, gb_ref, x_ref, w_ref, pg_ref, gate_ref, o_ref):
    wa = ma_ref.shape[-1]
    mb = (fy_ref[...] * _silu(gb_ref[...].astype(F32))).astype(BF16)
    y = jnp.dot(ma_ref[...], w_ref[:wa, :], preferred_element_type=F32)
    y = y + jnp.dot(mb, w_ref[wa:, :], preferred_element_type=F32)
    _post(y, x_ref, pg_ref, gate_ref, o_ref)


def _outproj_odd_kernel(m_ref, x_ref, w_ref, pg_ref, gate_ref, o_ref):
    y = jnp.dot(m_ref[...], w_ref[...], preferred_element_type=F32)
    _post(y, x_ref, pg_ref, gate_ref, o_ref)


def _outproj_even(ma, fy, u, x, w_out, pg, gate):
    b, t, d = x.shape
    wa = ma.shape[-1]
    wb = fy.shape[-1]
    tm = min(ROW_TILE, t)
    row = lambda bi, i: (bi, i, 0)
    return pl.pallas_call(
        _outproj_even_kernel,
        out_shape=jax.ShapeDtypeStruct((b, t, d), F32),
        grid=(b, t // tm),
        in_specs=[
            pl.BlockSpec((None, tm, wa), row),
            pl.BlockSpec((None, tm, wb), row),
            pl.BlockSpec((None, tm, wb), lambda bi, i: (bi, i, 3)),
            pl.BlockSpec((None, tm, d), row),
            pl.BlockSpec((wa + wb, d), lambda bi, i: (0, 0)),
            pl.BlockSpec((1, d), lambda bi, i: (0, 0)),
            pl.BlockSpec((None, 1, d), lambda bi, i: (bi, 0, 0)),
        ],
        out_specs=pl.BlockSpec((None, tm, d), row),
        compiler_params=_cparams(2),
        name="outproj_even",
    )(ma, fy, u, x, w_out, pg, gate)


def _outproj_odd(m, x, w_out, pg, gate):
    b, t, d = x.shape
    k = m.shape[-1]
    tm = min(ROW_TILE, t)
    row = lambda bi, i: (bi, i, 0)
    return pl.pallas_call(
        _outproj_odd_kernel,
        out_shape=jax.ShapeDtypeStruct((b, t, d), F32),
        grid=(b, t // tm),
        in_specs=[
            pl.BlockSpec((None, tm, k), row),
            pl.BlockSpec((None, tm, d), row),
            pl.BlockSpec((k, d), lambda bi, i: (0, 0)),
            pl.BlockSpec((1, d), lambda bi, i: (0, 0)),
            pl.BlockSpec((None, 1, d), lambda bi, i: (bi, 0, 0)),
        ],
        out_specs=pl.BlockSpec((None, tm, d), row),
        compiler_params=_cparams(2),
        name="outproj_odd",
    )(m, x, w_out, pg, gate)


def _ret_kernel(lg_ref, q_ref, k_ref, v_ref, *rest, reverse, final):
    if final:
        g_ref, op_ref, s0_ref, o_ref, se_ref, s_ref, dm_ref, qd_ref, kd_ref, cd_ref = rest
    else:
        s0_ref, o_ref, se_ref, s_ref, dm_ref, qd_ref, kd_ref, cd_ref = rest
    nh, dk, dv = s_ref.shape
    cr = q_ref.shape[0]
    d = 1 if reverse else 0

    @pl.when(pl.program_id(1) == 0)
    def _init():
        s_ref[...] = s0_ref[...]
        ii = lax.broadcasted_iota(jnp.int32, (cr, cr), 0).astype(F32)
        jj = lax.broadcasted_iota(jnp.int32, (cr, cr), 1).astype(F32)
        diff = jj - ii
        rq = lax.broadcasted_iota(jnp.int32, (cr, dv), 0).astype(F32)
        rk = lax.broadcasted_iota(jnp.int32, (cr, dk), 0).astype(F32)
        for h in range(nh):
            lgh = lg_ref[d, h]
            if reverse:
                mask = diff > 0
                expo = diff
                qe = cr - rq
                ke = rk
            else:
                mask = diff <= 0
                expo = -diff
                qe = rq + 1.0
                ke = (cr - 1.0) - rk
            dm_ref[h] = jnp.where(mask, jnp.exp(lgh * jnp.where(mask, expo, 0.0)), 0.0)
            qd_ref[h] = jnp.exp(lgh * qe)
            kd_ref[h] = jnp.exp(lgh * ke)
            cd_ref[h] = jnp.exp(jnp.full((dk, dv), lgh * cr, F32))

    for h in range(nh):
        qh = q_ref[:, h * dk:(h + 1) * dk]
        kh = k_ref[:, h * dk:(h + 1) * dk]
        vh = v_ref[:, h * dv:(h + 1) * dv]
        s = lax.dot_general(qh, kh, (((1,), (1,)), ((), ())), preferred_element_type=F32)
        o = jnp.dot((s * dm_ref[h]).astype(BF16), vh, preferred_element_type=F32)
        st = s_ref[h]
        o = o + qd_ref[h] * jnp.dot(qh, st.astype(BF16), preferred_element_type=F32)
        kdec = (kh.astype(F32) * kd_ref[h]).astype(BF16)
        s_ref[h] = cd_ref[h] * st + lax.dot_general(kdec, vh, (((0,), (0,)), ((), ())),
                                                     preferred_element_type=F32)
        if final:
            o = o + op_ref[:, h * dv:(h + 1) * dv].astype(F32)
            mu = jnp.mean(o, axis=-1, keepdims=True)
            oc = o - mu
            var = jnp.mean(oc * oc, axis=-1, keepdims=True)
            g = g_ref[:, h * dv:(h + 1) * dv].astype(F32)
            o = (oc * lax.rsqrt(var + EPS)) * _silu(g)
        o_ref[:, h * dv:(h + 1) * dv] = o.astype(o_ref.dtype)

    @pl.when(pl.program_id(1) == pl.num_programs(1) - 1)
    def _fin():
        se_ref[...] = s_ref[...]


def _ret_pass(u, lg, s0, o_prev, reverse, final):
    b, t, _ = u.shape
    _, nh, dk, dv = s0.shape
    cr = min(RET_CHUNK, t)
    nc = t // cr
    qk = nh * dk
    vw = nh * dv
    if reverse:
        ci = lambda i: nc - 1 - i
    else:
        ci = lambda i: i
    in_specs = [
        pl.BlockSpec(memory_space=pltpu.SMEM),
        pl.BlockSpec((None, cr, qk), lambda bi, i: (bi, ci(i), 0)),
        pl.BlockSpec((None, cr, qk), lambda bi, i: (bi, ci(i), 1)),
        pl.BlockSpec((None, cr, vw), lambda bi, i: (bi, ci(i), (2 * qk) // vw)),
    ]
    args = [lg, u, u, u]
    if final:
        in_specs += [pl.BlockSpec((None, cr, vw), lambda bi, i: (bi, ci(i), (2 * qk) // vw + 1)),
                     pl.BlockSpec((None, cr, vw), lambda bi, i: (bi, ci(i), 0))]
        args += [u, o_prev]
    in_specs.append(pl.BlockSpec((None, nh, dk, dv), lambda bi, i: (bi, 0, 0, 0)))
    args.append(s0)
    return pl.pallas_call(
        functools.partial(_ret_kernel, reverse=reverse, final=final),
        out_shape=(jax.ShapeDtypeStruct((b, t, vw), BF16),
                   jax.ShapeDtypeStruct((b, nh, dk, dv), F32)),
        grid=(b, nc),
        in_specs=in_specs,
        out_specs=(pl.BlockSpec((None, cr, vw), lambda bi, i: (bi, ci(i), 0)),
                   pl.BlockSpec((None, nh, dk, dv), lambda bi, i: (bi, 0, 0, 0))),
        scratch_shapes=[
            pltpu.VMEM((nh, dk, dv), F32),
            pltpu.VMEM((nh, cr, cr), F32),
            pltpu.VMEM((nh, cr, dv), F32),
            pltpu.VMEM((nh, cr, dk), F32),
            pltpu.VMEM((nh, dk, dv), F32),
        ],
        compiler_params=_cparams(2),
        name="retention_rev" if reverse else "retention_fwd",
    )(*args)


def _retention(u, uc, lg):
    b = u.shape[0]
    nh = RET_HEADS
    dk = u.shape[-1] // (6 * nh)
    zero = jnp.zeros((b, nh, dk, 2 * dk), F32)
    oc_r, s_r = _ret_pass(uc, lg, zero, None, True, False)
    mc, s_f = _ret_pass(uc, lg, zero, oc_r, False, True)
    o_r, _ = _ret_pass(u, lg, s_r, None, True, False)
    m, _ = _ret_pass(u, lg, s_f, o_r, False, True)
    return m, mc


def _rope_tables(t):
    nf = LANES // 2
    pos = jnp.arange(t, dtype=jnp.int32)
    row = (pos // GRID_W).astype(F32)
    col = (pos % GRID_W).astype(F32)
    inv = ROPE_THETA ** (-jnp.arange(nf, dtype=F32) / nf)
    ar = row[:, None] * inv
    ac = col[:, None] * inv
    cos_t = jnp.concatenate([jnp.cos(ar), jnp.cos(ar), jnp.cos(ac), jnp.cos(ac)], axis=1)
    sin_t = jnp.concatenate([-jnp.sin(ar), jnp.sin(ar), -jnp.sin(ac), jnp.sin(ac)], axis=1)
    return cos_t, sin_t


def kernel(x, c, ctx, c_ctx, mod_w, mod_b, pre_g, post_g, mix_w_in, mix_w_out, conv_w, conv_b,
           lru_wa, lru_ba, lru_wx, lru_bx, lru_lam, fnet_w, ret_w_in, ret_w_out, ret_log_gamma):
    b, t, d = x.shape
    tc = ctx.shape[1]
    depth = mod_w.shape[0]
    assert t % ROW_TILE == 0 and t % (FFT_N2 * FFT_J) == 0 and (t // FFT_N2) % FFT_J == 0
    assert tc % 16 == 0 and tc <= ROW_TILE and b + 1 <= 8

    cond8 = jnp.zeros((8, d), F32).at[:b].set(c).at[b].set(c_ctx)
    mods = _ada_mod(cond8, mod_w, mod_b)

    w12 = _fourier_weights(fnet_w)
    cos_t, sin_t = _rope_tables(t)
    cos_c = jnp.ones((tc, 2 * LANES), F32)
    sin_c = jnp.zeros((tc, 2 * LANES), F32)

    nh_l = LRU_HEADS
    hd = lru_wa.shape[-1]
    wg_all = jnp.concatenate([lru_wa, lru_wx], axis=-1).astype(BF16)
    bg_all = jnp.concatenate([lru_ba.reshape(-1, 2, nh_l, 1, hd),
                              lru_bx.reshape(-1, 2, nh_l, 1, hd)], axis=-1)
    lam_all = lru_lam.reshape(-1, 2, nh_l, 1, hd)

    qk = ret_w_in.shape[-1] // 6
    kscale = jnp.concatenate([jnp.ones((qk,), F32), jnp.full((qk,), (qk // RET_HEADS) ** -0.5, F32),
                              jnp.ones((4 * qk,), F32)])

    xc = ctx
    for layer in range(depth):
        need_ctx = layer < depth - 1
        m_l = mods[layer]
        shift, scale, gate = m_l[:, :d], m_l[:, d:2 * d], m_l[:, 2 * d:]
        sc_l, sh_l, gt_l = (v[:b].reshape(b, 1, d) for v in (scale, shift, gate))
        sc_c, sh_c, gt_c = (jnp.broadcast_to(v[b].reshape(1, 1, d), (b, 1, d)) for v in (scale, shift, gate))
        pre = pre_g[layer].reshape(1, d)
        post = post_g[layer].reshape(1, d)
        if layer % 2 == 0:
            e = layer // 2
            w_in = mix_w_in[e].astype(BF16)
            w_out = mix_w_out[e].astype(BF16)
            u = _inproj(x, pre, sc_l, sh_l, w_in)
            uc = _inproj(xc, pre, sc_c, sh_c, w_in)
            ma, mca = _lru_branch(u, uc, conv_w[e], conv_b[e].reshape(1, -1),
                                  wg_all[e], bg_all[e], lam_all[e])
            ur, ui = _chanmix(u, w12[e])
            fy = _seq_dft(ur, ui)
            x = _outproj_even(ma, fy, u, x, w_out, post, gt_l)
            if need_ctx:
                urc, uic = _chanmix(uc, w12[e])
                fyc = _seq_dft_dense(urc, uic)
                xc = _outproj_even(mca, fyc, uc, xc, w_out, post, gt_c)
        else:
            j = layer // 2
            w_in = (ret_w_in[j] * kscale).astype(BF16)
            w_out = ret_w_out[j].astype(BF16)
            u = _inproj(x, pre, sc_l, sh_l, w_in, rope=(cos_t, sin_t, 2 * qk))
            uc = _inproj(xc, pre, sc_c, sh_c, w_in, rope=(cos_c, sin_c, 2 * qk))
            m, mc = _retention(u, uc, ret_log_gamma[j])
            x = _outproj_odd(m, x, w_out, post, gt_l)
            if need_ctx:
                xc = _outproj_odd(mc, xc, w_out, post, gt_c)
    return x
```

```python
import functools
import math

import numpy as np
import jax
import jax.numpy as jnp
from jax import lax
from jax.experimental import pallas as pl
from jax.experimental.pallas import tpu as pltpu

F32 = jnp.float32
BF16 = jnp.bfloat16

EPS = 1e-6
LOG2E = math.log2(math.e)
LRU_C = 8.0
LRU_HEADS = 8
FNET_GROUPS = 4
RET_HEADS = 4
GRID_W = 64
ROPE_THETA = 10000.0

LANES = 128
VMEM_LIMIT = 56 * 1024 * 1024

ROW_TILE = 512
LRU_CHUNK = 256
RET_CHUNK = 256
FFT_N2 = 128
FFT_J = 16


def _cparams(n_axes):
    return pltpu.CompilerParams(dimension_semantics=("arbitrary",) * n_axes,
                                vmem_limit_bytes=VMEM_LIMIT)


def _silu(v):
    return v * jax.nn.sigmoid(v)


def _ada_kernel(cond_ref, w_ref, b_ref, o_ref):
    s = _silu(cond_ref[...])
    o_ref[...] = jnp.dot(s, w_ref[...], preferred_element_type=F32,
                         precision=lax.Precision.HIGHEST) + b_ref[...]


def _ada_mod(cond8, mod_w, mod_b):
    depth, d, n3 = mod_w.shape
    tn = d
    return pl.pallas_call(
        _ada_kernel,
        out_shape=jax.ShapeDtypeStruct((depth, 8, n3), F32),
        grid=(depth, n3 // tn),
        in_specs=[
            pl.BlockSpec((8, d), lambda l, j: (0, 0)),
            pl.BlockSpec((None, d, tn), lambda l, j: (l, 0, j)),
            pl.BlockSpec((None, 1, tn), lambda l, j: (l, 0, j)),
        ],
        out_specs=pl.BlockSpec((None, 8, tn), lambda l, j: (l, 0, j)),
        compiler_params=_cparams(2),
        name="ada_mod",
    )(cond8, mod_w, mod_b.reshape(depth, 1, n3))


def _inproj_kernel(x_ref, g_ref, sc_ref, sh_ref, w_ref, *rest, rope_cols, chunk):
    if rope_cols:
        cos_ref, sin_ref, o_ref = rest
    else:
        (o_ref,) = rest
    x = x_ref[...]
    ms = jnp.mean(x * x, axis=-1, keepdims=True)
    h = (x * lax.rsqrt(ms + EPS)) * g_ref[...]
    h = h * (1.0 + sc_ref[...]) + sh_ref[...]
    hb = h.astype(BF16)
    n_out = o_ref.shape[-1]
    for c0 in range(0, n_out, chunk):
        r = jnp.dot(hb, w_ref[:, c0:c0 + chunk], preferred_element_type=F32)
        if c0 < rope_cols:
            parts = []
            for s0 in range(0, chunk, LANES):
                col = (c0 + s0) % (2 * LANES)
                ra = r[:, s0:s0 + LANES]
                parts.append(ra * cos_ref[:, col:col + LANES]
                             + pltpu.roll(ra, LANES // 2, axis=1) * sin_ref[:, col:col + LANES])
            r = jnp.concatenate(parts, axis=1)
        o_ref[:, c0:c0 + chunk] = r.astype(o_ref.dtype)


def _inproj(x, g, scale, shift, w, rope=None):
    b, t, d = x.shape
    n = w.shape[1]
    tm = min(ROW_TILE, t)
    in_specs = [
        pl.BlockSpec((None, tm, d), lambda bi, i: (bi, i, 0)),
        pl.BlockSpec((1, d), lambda bi, i: (0, 0)),
        pl.BlockSpec((None, 1, d), lambda bi, i: (bi, 0, 0)),
        pl.BlockSpec((None, 1, d), lambda bi, i: (bi, 0, 0)),
        pl.BlockSpec((d, n), lambda bi, i: (0, 0), pipeline_mode=pl.Buffered(1)),
    ]
    args = [x, g, scale, shift, w]
    rope_cols = 0
    if rope is not None:
        cos_t, sin_t, rope_cols = rope
        in_specs += [pl.BlockSpec((tm, 2 * LANES), lambda bi, i: (i, 0)),
                     pl.BlockSpec((tm, 2 * LANES), lambda bi, i: (i, 0))]
        args += [cos_t, sin_t]
    return pl.pallas_call(
        functools.partial(_inproj_kernel, rope_cols=rope_cols, chunk=2 * LANES),
        out_shape=jax.ShapeDtypeStruct((b, t, n), BF16),
        grid=(b, t // tm),
        in_specs=in_specs,
        out_specs=pl.BlockSpec((None, tm, n), lambda bi, i: (bi, i, 0)),
        compiler_params=_cparams(2),
        name="inproj",
    )(*args)


def _scan_chunk(a, b, carry, reverse):
    rows, lanes = a.shape
    ng = rows // 8
    a = a.reshape(ng, 8, lanes)
    b = b.reshape(ng, 8, lanes)
    sub = lax.broadcasted_iota(jnp.int32, (1, 8, lanes), 1)
    for dd in (1, 2, 4):
        if reverse:
            m = sub <= 7 - dd
            sh = 8 - dd
        else:
            m = sub >= dd
            sh = dd
        a_sh = pltpu.roll(a, sh, axis=1)
        b_sh = pltpu.roll(b, sh, axis=1)
        b = b + a * jnp.where(m, b_sh, 0.0)
        a = a * jnp.where(m, a_sh, 1.0)
    outs = [None] * ng
    order = range(ng - 1, -1, -1) if reverse else range(ng)
    for gi in order:
        hg = b[gi] + a[gi] * carry
        carry = hg[0:1] if reverse else hg[7:8]
        outs[gi] = hg
    return jnp.concatenate(outs, axis=0), carry


def _sqrt_nonneg(v):
    return jnp.where(v > 0.0, v * lax.rsqrt(v), 0.0)


def _lru_kernel(xa_ref, ga_ref, xca_ref, gca_ref, cw_ref, cb_ref, wg_ref, bg_ref, lam_ref,
                o_ref, oc_ref, zs_ref, hb_ref):
    cw = cw_ref[...]
    cb = cb_ref[...]
    nsp2 = []
    for d in (0, 1):
        y = -lam_ref[d]
        nsp2.append((-LRU_C * LOG2E) * (jnp.maximum(y, 0.0) + jnp.log1p(jnp.exp(-jnp.abs(y)))))

    def run_seq(x_ref, g_ref, out_ref, carry_f, carry_r):
        tn = x_ref.shape[0]
        rn = min(LRU_CHUNK, tn)
        nch = tn // rn
        assert nch == 1 or nch % 2 == 0

        def conv_body(c, carry):
            r0 = pl.multiple_of(c * rn, rn)
            p0 = pl.multiple_of(jnp.maximum(r0 - 16, 0), 16)
            n0 = pl.multiple_of(jnp.minimum(r0 + rn, tn - 16), 16)
            cur = x_ref[pl.ds(r0, rn), :].astype(F32)
            prev = jnp.where(c > 0, x_ref[pl.ds(p0, 16), :].astype(F32), 0.0)
            nxt = jnp.where(c < nch - 1, x_ref[pl.ds(n0, 16), :].astype(F32), 0.0)
            ext = jnp.concatenate([prev, cur, nxt], axis=0)
            z = cb + cw[0:1] * pltpu.roll(ext, 2, axis=0)[16:16 + rn]
            z = z + cw[1:2] * pltpu.roll(ext, 1, axis=0)[16:16 + rn]
            z = z + cw[2:3] * cur
            z = z + cw[3:4] * pltpu.roll(ext, rn + 31, axis=0)[16:16 + rn]
            zs_ref[pl.ds(r0, rn), :] = z
            return carry

        lax.fori_loop(0, nch, conv_body, 0)

        def coeffs(r0, d):
            z = zs_ref[pl.ds(r0, rn), :]
            e = jnp.exp2(jnp.dot(z.astype(BF16), wg_ref[d], preferred_element_type=F32) + bg_ref[d])
            r = 1.0 / (1.0 + e[:, :LANES])
            i = 1.0 / (1.0 + e[:, LANES:])
            a = jnp.exp2(nsp2[d] * r)
            return a, _sqrt_nonneg(1.0 - a * a) * (i * z)

        def gate(r0):
            return _silu(g_ref[pl.ds(r0, rn), :].astype(F32))

        if nch == 1:
            a, b = coeffs(0, 0)
            hf, carry_f = _scan_chunk(a, b, carry_f, False)
            a, b = coeffs(0, 1)
            hr, carry_r = _scan_chunk(a, b, carry_r, True)
            out_ref[...] = ((hf + hr) * gate(0)).astype(out_ref.dtype)
            return carry_f, carry_r

        def step(c, carry, combine):
            cf, cr = carry
            r0 = pl.multiple_of(c * rn, rn)
            r1 = pl.multiple_of((nch - 1 - c) * rn, rn)
            a, b = coeffs(r0, 0)
            hf, cf = _scan_chunk(a, b, cf, False)
            a, b = coeffs(r1, 1)
            hr, cr = _scan_chunk(a, b, cr, True)
            if combine:
                out_ref[pl.ds(r0, rn), :] = ((hf + hb_ref[pl.ds(r0, rn), :]) * gate(r0)).astype(out_ref.dtype)
                out_ref[pl.ds(r1, rn), :] = ((hb_ref[pl.ds(r1, rn), :] + hr) * gate(r1)).astype(out_ref.dtype)
            else:
                hb_ref[pl.ds(r0, rn), :] = hf
                hb_ref[pl.ds(r1, rn), :] = hr
            return cf, cr

        carry = lax.fori_loop(0, nch // 2, functools.partial(step, combine=False), (carry_f, carry_r))
        return lax.fori_loop(nch // 2, nch, functools.partial(step, combine=True), carry)

    zero = jnp.zeros((1, LANES), F32)
    cf, cr = run_seq(xca_ref, gca_ref, oc_ref, zero, zero)
    run_seq(xa_ref, ga_ref, o_ref, cf, cr)


def _lru_branch(u, uc, conv_w, conv_b, wg, bg, lam):
    b, t, _ = u.shape
    tc = uc.shape[1]
    w = conv_w.shape[1]
    nh = w // LANES
    ga0 = 2 * nh
    return pl.pallas_call(
        _lru_kernel,
        out_shape=(jax.ShapeDtypeStruct((b, t, w), BF16), jax.ShapeDtypeStruct((b, tc, w), BF16)),
        grid=(b, nh),
        in_specs=[
            pl.BlockSpec((None, t, LANES), lambda bi, h: (bi, 0, h)),
            pl.BlockSpec((None, t, LANES), lambda bi, h: (bi, 0, ga0 + h)),
            pl.BlockSpec((None, tc, LANES), lambda bi, h: (bi, 0, h)),
            pl.BlockSpec((None, tc, LANES), lambda bi, h: (bi, 0, ga0 + h)),
            pl.BlockSpec((conv_w.shape[0], LANES), lambda bi, h: (0, h)),
            pl.BlockSpec((1, LANES), lambda bi, h: (0, h)),
            pl.BlockSpec((2, None, LANES, 2 * LANES), lambda bi, h: (0, h, 0, 0)),
            pl.BlockSpec((2, None, 1, 2 * LANES), lambda bi, h: (0, h, 0, 0)),
            pl.BlockSpec((2, None, 1, LANES), lambda bi, h: (0, h, 0, 0)),
        ],
        out_specs=(pl.BlockSpec((None, t, LANES), lambda bi, h: (bi, 0, h)),
                   pl.BlockSpec((None, tc, LANES), lambda bi, h: (bi, 0, h))),
        scratch_shapes=[pltpu.VMEM((t, LANES), F32), pltpu.VMEM((t, LANES), F32)],
        compiler_params=_cparams(2),
        name="rglru",
    )(u, u, uc, uc, conv_w, conv_b, wg, bg, lam)


def _w12_kernel(cs_ref, wf_ref, o_ref):
    gd = wf_ref.shape[0]
    for half in range(gd // LANES):
        wf = wf_ref[:, half * LANES:(half + 1) * LANES]
        o_ref[half, :, :LANES] = jnp.dot(cs_ref[:gd, :], wf, preferred_element_type=F32,
                                         precision=lax.Precision.HIGHEST).astype(o_ref.dtype)
        o_ref[half, :, LANES:] = jnp.dot(cs_ref[gd:, :], wf, preferred_element_type=F32,
                                         precision=lax.Precision.HIGHEST).astype(o_ref.dtype)


def _fourier_weights(fnet_w):
    ne, ng, gd, _ = fnet_w.shape
    nhalf = gd // LANES
    idx = np.arange(gd)
    ang = 2.0 * np.pi * ((idx[:, None] * idx[None, :]) % gd) / gd
    cs = np.concatenate([np.cos(ang), np.sin(ang)], axis=0) / math.sqrt(gd)
    out = pl.pallas_call(
        _w12_kernel,
        out_shape=jax.ShapeDtypeStruct((ne, ng, nhalf, gd, 2 * LANES), BF16),
        grid=(ne, ng),
        in_specs=[pl.BlockSpec((2 * gd, gd), lambda e, g: (0, 0)),
                  pl.BlockSpec((None, None, gd, gd), lambda e, g: (e, g, 0, 0))],
        out_specs=pl.BlockSpec((None, None, nhalf, gd, 2 * LANES), lambda e, g: (e, g, 0, 0, 0)),
        compiler_params=_cparams(2),
        name="fourier_weights",
    )(jnp.asarray(cs, F32), fnet_w)
    return out.reshape(ne, ng * nhalf, gd, 2 * LANES)


def _chanmix_kernel(xb_ref, w_ref, ur_ref, ui_ref):
    ncb, gd, _ = w_ref.shape
    per_group = gd // LANES
    for cb in range(ncb):
        g = cb // per_group
        r = jnp.dot(xb_ref[:, g * gd:(g + 1) * gd], w_ref[cb], preferred_element_type=F32)
        ur_ref[:, cb * LANES:(cb + 1) * LANES] = r[:, :LANES]
        ui_ref[:, cb * LANES:(cb + 1) * LANES] = r[:, LANES:]


def _chanmix(u, w12):
    b, t, _ = u.shape
    ncb, gd, _ = w12.shape
    w = ncb * LANES
    tm = min(ROW_TILE, t)
    out = jax.ShapeDtypeStruct((b, t, w), F32)
    return pl.pallas_call(
        _chanmix_kernel,
        out_shape=(out, out),
        grid=(b, t // tm),
        in_specs=[pl.BlockSpec((None, tm, w), lambda bi, i: (bi, i, 1)),
                  pl.BlockSpec((ncb, gd, 2 * LANES), lambda bi, i: (0, 0, 0))],
        out_specs=(pl.BlockSpec((None, tm, w), lambda bi, i: (bi, i, 0)),
                   pl.BlockSpec((None, tm, w), lambda bi, i: (bi, i, 0))),
        compiler_params=_cparams(2),
        name="fourier_chanmix",
    )(u, w12)


def _dft_a_kernel(xb_ref, w_ref, ea_ref, twr_ref, twi_ref, br_ref, bi_ref, ur_s, ui_s):
    n2, nj, gd = xb_ref.shape
    u = jnp.dot(xb_ref[...].reshape(n2 * nj, gd), w_ref[...], preferred_element_type=F32)
    ur_s[...] = u[:, :LANES]
    ui_s[...] = u[:, LANES:]
    xr = jnp.concatenate([ur_s[pl.ds(j, n2, stride=nj), :] for j in range(nj)], axis=1)
    xi = jnp.concatenate([ui_s[pl.ds(j, n2, stride=nj), :] for j in range(nj)], axis=1)
    rhs = jnp.concatenate([xr, xi], axis=0).astype(BF16)
    res = jnp.dot(ea_ref[...], rhs, preferred_element_type=F32)
    for j in range(nj):
        pr = res[:n2, j * LANES:(j + 1) * LANES]
        pi = res[n2:, j * LANES:(j + 1) * LANES]
        tr = twr_ref[j]
        ti = twi_ref[j]
        br_ref[j] = pr * tr - pi * ti
        bi_ref[j] = pr * ti + pi * tr


def _dft_b_kernel(br_ref, bi_ref, eb_ref, o_ref, sr_ref, si_ref):
    n1, nj, _ = br_ref.shape
    sr_ref[...] = br_ref[...].reshape(n1 * nj, LANES)
    si_ref[...] = bi_ref[...].reshape(n1 * nj, LANES)
    xr = jnp.concatenate([sr_ref[pl.ds(j, n1, stride=nj), :] for j in range(nj)], axis=1)
    xi = jnp.concatenate([si_ref[pl.ds(j, n1, stride=nj), :] for j in range(nj)], axis=1)
    rhs = jnp.concatenate([xr, xi], axis=0).astype(BF16)
    res = jnp.dot(eb_ref[...], rhs, preferred_element_type=F32)
    for j in range(nj):
        o_ref[:, j, :] = res[:, j * LANES:(j + 1) * LANES]


def _dft_consts(t):
    n2 = FFT_N2
    n1 = t // n2
    i2 = np.arange(n2)
    a2 = 2.0 * np.pi * ((i2[:, None] * i2[None, :]) % n2) / n2
    er, ei = np.cos(a2), np.sin(a2)
    ea = np.block([[er, -ei], [ei, er]])
    i1 = np.arange(n1)
    atw = 2.0 * np.pi * ((i1[:, None] * i2[None, :]) % t) / t
    a1 = 2.0 * np.pi * ((i1[:, None] * i1[None, :]) % n1) / n1
    eb = np.concatenate([np.cos(a1), -np.sin(a1)], axis=1) / math.sqrt(t)
    twr = jnp.broadcast_to(jnp.asarray(np.cos(atw), F32)[:, :, None], (n1, n2, LANES))
    twi = jnp.broadcast_to(jnp.asarray(np.sin(atw), F32)[:, :, None], (n1, n2, LANES))
    return jnp.asarray(ea, F32).astype(BF16), twr, twi, jnp.asarray(eb, F32).astype(BF16)


def _seq_fourier(u, w12, consts):
    b, t, n_in = u.shape
    ncb, gd, _ = w12.shape
    w = ncb * LANES
    n2 = FFT_N2
    n1 = t // n2
    nj = FFT_J
    ea, twr, twi, eb = consts
    u4 = u.reshape(b, n2, n1, n_in)
    xb0 = w // gd
    per_group = gd // LANES
    mid = jax.ShapeDtypeStruct((b, n1, n2, w), F32)
    br, bi = pl.pallas_call(
        _dft_a_kernel,
        out_shape=(mid, mid),
        grid=(n1 // nj, b, ncb),
        in_specs=[
            pl.BlockSpec((None, n2, nj, gd), lambda tb, bi_, cb: (bi_, 0, tb, xb0 + cb // per_group)),
            pl.BlockSpec((None, gd, 2 * LANES), lambda tb, bi_, cb: (cb, 0, 0)),
            pl.BlockSpec((2 * n2, 2 * n2), lambda tb, bi_, cb: (0, 0)),
            pl.BlockSpec((nj, n2, LANES), lambda tb, bi_, cb: (tb, 0, 0)),
            pl.BlockSpec((nj, n2, LANES), lambda tb, bi_, cb: (tb, 0, 0)),
        ],
        out_specs=(pl.BlockSpec((None, nj, n2, LANES), lambda tb, bi_, cb: (bi_, tb, 0, cb)),
                   pl.BlockSpec((None, nj, n2, LANES), lambda tb, bi_, cb: (bi_, tb, 0, cb))),
        scratch_shapes=[pltpu.VMEM((n2 * nj, LANES), F32), pltpu.VMEM((n2 * nj, LANES), F32)],
        compiler_params=_cparams(3),
        name="dft_stage_a",
    )(u4, w12, ea, twr, twi)
    out = pl.pallas_call(
        _dft_b_kernel,
        out_shape=jax.ShapeDtypeStruct((b, n1, n2, w), F32),
        grid=(b, n2 // nj, w // LANES),
        in_specs=[
            pl.BlockSpec((None, n1, nj, LANES), lambda bi_, kb, cb: (bi_, 0, kb, cb)),
            pl.BlockSpec((None, n1, nj, LANES), lambda bi_, kb, cb: (bi_, 0, kb, cb)),
            pl.BlockSpec((n1, 2 * n1), lambda bi_, kb, cb: (0, 0)),
        ],
        out_specs=pl.BlockSpec((None, n1, nj, LANES), lambda bi_, kb, cb: (bi_, 0, kb, cb)),
        scratch_shapes=[pltpu.VMEM((n1 * nj, LANES), F32), pltpu.VMEM((n1 * nj, LANES), F32)],
        compiler_params=_cparams(3),
        name="dft_stage_b",
    )(br, bi, eb)
    return out.reshape(b, t, w)


def _dft_dense_kernel(ur_ref, ui_ref, ec_ref, o_ref):
    rhs = jnp.concatenate([ur_ref[...], ui_ref[...]], axis=0).astype(BF16)
    o_ref[...] = jnp.dot(ec_ref[...], rhs, preferred_element_type=F32)


def _seq_dft_dense(ur, ui):
    b, t, w = ur.shape
    it = np.arange(t)
    ang = 2.0 * np.pi * ((it[:, None] * it[None, :]) % t) / t
    ec = jnp.asarray(np.concatenate([np.cos(ang), -np.sin(ang)], axis=1) / math.sqrt(t), F32).astype(BF16)
    return pl.pallas_call(
        _dft_dense_kernel,
        out_shape=jax.ShapeDtypeStruct((b, t, w), F32),
        grid=(b,),
        in_specs=[pl.BlockSpec((None, t, w), lambda bi: (bi, 0, 0)),
                  pl.BlockSpec((None, t, w), lambda bi: (bi, 0, 0)),
                  pl.BlockSpec((t, 2 * t), lambda bi: (0, 0))],
        out_specs=pl.BlockSpec((None, t, w), lambda bi: (bi, 0, 0)),
        compiler_params=_cparams(1),
        name="dft_dense",
    )(ur, ui, ec)


def _post(y, x_ref, pg_ref, gate_ref, o_ref):
    ms = jnp.mean(y * y, axis=-1, keepdims=True)
    yn = (y * lax.rsqrt(ms + EPS)) * pg_ref[...]
    o_ref[...] = x_ref[...] + gate_ref[...] * yn


def _outproj_even_kernel(ma_ref, fy_ref, gb_ref, x_ref, w_ref, pg_ref, gate_ref, o_ref):
    wa = ma_ref.shape[-1]
    mb = (fy_ref[...] * _silu(gb_ref[...].astype(F32))).astype(BF16)
    y = jnp.dot(ma_ref[...], w_ref[:wa, :], preferred_element_type=F32)
    y = y + jnp.dot(mb, w_ref[wa:, :], preferred_element_type=F32)
    _post(y, x_ref, pg_ref, gate_ref, o_ref)


def _outproj_odd_kernel(on_ref, g_ref, x_ref, w_ref, pg_ref, gate_ref, o_ref):
    m = (on_ref[...].astype(F32) * _silu(g_ref[...].astype(F32))).astype(BF16)
    y = jnp.dot(m, w_ref[...], preferred_element_type=F32)
    _post(y, x_ref, pg_ref, gate_ref, o_ref)


def _outproj_even(ma, fy, u, x, w_out, pg, gate):
    b, t, d = x.shape
    wa = ma.shape[-1]
    wb = fy.shape[-1]
    tm = min(ROW_TILE, t)
    row = lambda bi, i: (bi, i, 0)
    return pl.pallas_call(
        _outproj_even_kernel,
        out_shape=jax.ShapeDtypeStruct((b, t, d), F32),
        grid=(b, t // tm),
        in_specs=[
            pl.BlockSpec((None, tm, wa), row),
            pl.BlockSpec((None, tm, wb), row),
            pl.BlockSpec((None, tm, wb), lambda bi, i: (bi, i, 3)),
            pl.BlockSpec((None, tm, d), row),
            pl.BlockSpec((wa + wb, d), lambda bi, i: (0, 0)),
            pl.BlockSpec((1, d), lambda bi, i: (0, 0)),
            pl.BlockSpec((None, 1, d), lambda bi, i: (bi, 0, 0)),
        ],
        out_specs=pl.BlockSpec((None, tm, d), row),
        compiler_params=_cparams(2),
        name="outproj_even",
    )(ma, fy, u, x, w_out, pg, gate)


def _outproj_odd(on, u, x, w_out, pg, gate):
    b, t, d = x.shape
    k = on.shape[-1]
    g_blk = u.shape[-1] // k - 1
    tm = min(ROW_TILE, t)
    row = lambda bi, i: (bi, i, 0)
    return pl.pallas_call(
        _outproj_odd_kernel,
        out_shape=jax.ShapeDtypeStruct((b, t, d), F32),
        grid=(b, t // tm),
        in_specs=[
            pl.BlockSpec((None, tm, k), row),
            pl.BlockSpec((None, tm, k), lambda bi, i: (bi, i, g_blk)),
            pl.BlockSpec((None, tm, d), row),
            pl.BlockSpec((k, d), lambda bi, i: (0, 0)),
            pl.BlockSpec((1, d), lambda bi, i: (0, 0)),
            pl.BlockSpec((None, 1, d), lambda bi, i: (bi, 0, 0)),
        ],
        out_specs=pl.BlockSpec((None, tm, d), row),
        compiler_params=_cparams(2),
        name="outproj_odd",
    )(on, u, x, w_out, pg, gate)


def _ret_kernel(lg_ref, q_ref, k_ref, v_ref, *rest, reverse, final):
    if final:
        op_ref, s0_ref, o_ref, se_ref, s_ref, dm_ref, qd_ref, kd_ref = rest
    else:
        s0_ref, o_ref, se_ref, s_ref, dm_ref, qd_ref, kd_ref = rest
    nh, dk, dv = s_ref.shape
    cr = q_ref.shape[0]
    d = 1 if reverse else 0

    @pl.when(pl.program_id(1) == 0)
    def _init():
        s_ref[...] = s0_ref[...]
        ii = lax.broadcasted_iota(jnp.int32, (cr, cr), 0).astype(F32)
        jj = lax.broadcasted_iota(jnp.int32, (cr, cr), 1).astype(F32)
        diff = jj - ii
        rq = lax.broadcasted_iota(jnp.int32, (cr, dv), 0).astype(F32)
        rk = lax.broadcasted_iota(jnp.int32, (cr, dk), 0).astype(F32)
        for h in range(nh):
            lgh = lg_ref[d, h]
            if reverse:
                mask = diff > 0
                expo = diff
                qe = cr - rq
                ke = rk
            else:
                mask = diff <= 0
                expo = -diff
                qe = rq + 1.0
                ke = (cr - 1.0) - rk
            dm_ref[h] = jnp.where(mask, jnp.exp(lgh * jnp.where(mask, expo, 0.0)), 0.0)
            qd_ref[h] = jnp.exp(lgh * qe)
            kd_ref[h] = jnp.exp(lgh * ke)

    for h in range(nh):
        cdec = jnp.exp(jnp.full((1, dv), lg_ref[d, h] * cr, F32))
        qh = q_ref[:, h * dk:(h + 1) * dk]
        kh = k_ref[:, h * dk:(h + 1) * dk]
        vh = v_ref[:, h * dv:(h + 1) * dv]
        s = lax.dot_general(qh, kh, (((1,), (1,)), ((), ())), preferred_element_type=F32)
        o = jnp.dot((s * dm_ref[h]).astype(BF16), vh, preferred_element_type=F32)
        st = s_ref[h]
        o = o + qd_ref[h] * jnp.dot(qh, st.astype(BF16), preferred_element_type=F32)
        kdec = (kh.astype(F32) * kd_ref[h]).astype(BF16)
        s_ref[h] = cdec * st + lax.dot_general(kdec, vh, (((0,), (0,)), ((), ())),
                                               preferred_element_type=F32)
        if final:
            o = o + op_ref[:, h * dv:(h + 1) * dv].astype(F32)
            mu = jnp.mean(o, axis=-1, keepdims=True)
            oc = o - mu
            var = jnp.mean(oc * oc, axis=-1, keepdims=True)
            o = oc * lax.rsqrt(var + EPS)
        o_ref[:, h * dv:(h + 1) * dv] = o.astype(o_ref.dtype)

    @pl.when(pl.program_id(1) == pl.num_programs(1) - 1)
    def _fin():
        se_ref[...] = s_ref[...]


def _ret_pass(u, lg, s0, o_prev, reverse, final):
    b, t, _ = u.shape
    _, nh, dk, dv = s0.shape
    cr = min(RET_CHUNK, t)
    nc = t // cr
    qk = nh * dk
    vw = nh * dv
    if reverse:
        ci = lambda i: nc - 1 - i
    else:
        ci = lambda i: i
    in_specs = [
        pl.BlockSpec(memory_space=pltpu.SMEM),
        pl.BlockSpec((None, cr, qk), lambda bi, i: (bi, ci(i), 0)),
        pl.BlockSpec((None, cr, qk), lambda bi, i: (bi, ci(i), 1)),
        pl.BlockSpec((None, cr, vw), lambda bi, i: (bi, ci(i), (2 * qk) // vw)),
    ]
    args = [lg, u, u, u]
    if final:
        in_specs.append(pl.BlockSpec((None, cr, vw), lambda bi, i: (bi, ci(i), 0)))
        args.append(o_prev)
    in_specs.append(pl.BlockSpec((None, nh, dk, dv), lambda bi, i: (bi, 0, 0, 0)))
    args.append(s0)
    return pl.pallas_call(
        functools.partial(_ret_kernel, reverse=reverse, final=final),
        out_shape=(jax.ShapeDtypeStruct((b, t, vw), BF16),
                   jax.ShapeDtypeStruct((b, nh, dk, dv), F32)),
        grid=(b, nc),
        in_specs=in_specs,
        out_specs=(pl.BlockSpec((None, cr, vw), lambda bi, i: (bi, ci(i), 0)),
                   pl.BlockSpec((None, nh, dk, dv), lambda bi, i: (bi, 0, 0, 0))),
        scratch_shapes=[
            pltpu.VMEM((nh, dk, dv), F32),
            pltpu.VMEM((nh, cr, cr), F32),
            pltpu.VMEM((nh, cr, dv), F32),
            pltpu.VMEM((nh, cr, dk), F32),
        ],
        compiler_params=_cparams(2),
        name="retention_rev" if reverse else "retention_fwd",
    )(*args)


def _retention(u, uc, lg):
    b = u.shape[0]
    nh = RET_HEADS
    dk = u.shape[-1] // (6 * nh)
    zero = jnp.zeros((b, nh, dk, 2 * dk), F32)
    oc_r, s_r = _ret_pass(uc, lg, zero, None, True, False)
    mc, s_f = _ret_pass(uc, lg, zero, oc_r, False, True)
    o_r, _ = _ret_pass(u, lg, s_r, None, True, False)
    m, _ = _ret_pass(u, lg, s_f, o_r, False, True)
    return m, mc


def _rope_tables(t):
    nf = LANES // 2
    rows = t // GRID_W
    inv = ROPE_THETA ** (-jnp.arange(nf, dtype=F32) / nf)
    ar = jnp.arange(rows, dtype=F32)[:, None] * inv
    ac = jnp.arange(GRID_W, dtype=F32)[:, None] * inv
    exp_r = lambda v: jnp.repeat(v, GRID_W, axis=0)
    exp_c = lambda v: jnp.tile(v, (rows, 1))
    cr, sr, cc, sc = exp_r(jnp.cos(ar)), exp_r(jnp.sin(ar)), exp_c(jnp.cos(ac)), exp_c(jnp.sin(ac))
    cos_t = jnp.concatenate([cr, cr, cc, cc], axis=1)
    sin_t = jnp.concatenate([-sr, sr, -sc, sc], axis=1)
    return cos_t, sin_t


def kernel(x, c, ctx, c_ctx, mod_w, mod_b, pre_g, post_g, mix_w_in, mix_w_out, conv_w, conv_b,
           lru_wa, lru_ba, lru_wx, lru_bx, lru_lam, fnet_w, ret_w_in, ret_w_out, ret_log_gamma):
    b, t, d = x.shape
    tc = ctx.shape[1]
    depth = mod_w.shape[0]
    assert t % ROW_TILE == 0 and t % (FFT_N2 * FFT_J) == 0 and (t // FFT_N2) % FFT_J == 0
    assert tc % 16 == 0 and tc <= ROW_TILE and b + 1 <= 8

    cond8 = jnp.zeros((8, d), F32).at[:b].set(c).at[b].set(c_ctx)
    mods = _ada_mod(cond8, mod_w, mod_b)

    w12 = _fourier_weights(fnet_w)
    dft_consts = _dft_consts(t)
    cos_t, sin_t = _rope_tables(t)
    cos_c = jnp.ones((tc, 2 * LANES), F32)
    sin_c = jnp.zeros((tc, 2 * LANES), F32)

    nh_l = LRU_HEADS
    hd = lru_wa.shape[-1]
    wg_all = (jnp.concatenate([lru_wa, lru_wx], axis=-1) * (-LOG2E)).astype(BF16)
    bg_all = jnp.concatenate([lru_ba.reshape(-1, 2, nh_l, 1, hd),
                              lru_bx.reshape(-1, 2, nh_l, 1, hd)], axis=-1) * (-LOG2E)
    lam_all = lru_lam.reshape(-1, 2, nh_l, 1, hd)

    qk = ret_w_in.shape[-1] // 6
    kscale = jnp.concatenate([jnp.ones((qk,), F32), jnp.full((qk,), (qk // RET_HEADS) ** -0.5, F32),
                              jnp.ones((4 * qk,), F32)])

    xc = ctx
    for layer in range(depth):
        need_ctx = layer < depth - 1
        m_l = mods[layer]
        shift, scale, gate = m_l[:, :d], m_l[:, d:2 * d], m_l[:, 2 * d:]
        sc_l, sh_l, gt_l = (v[:b].reshape(b, 1, d) for v in (scale, shift, gate))
        sc_c, sh_c, gt_c = (jnp.broadcast_to(v[b].reshape(1, 1, d), (b, 1, d)) for v in (scale, shift, gate))
        pre = pre_g[layer].reshape(1, d)
        post = post_g[layer].reshape(1, d)
        if layer % 2 == 0:
            e = layer // 2
            w_in = mix_w_in[e].astype(BF16)
            w_out = mix_w_out[e].astype(BF16)
            u = _inproj(x, pre, sc_l, sh_l, w_in)
            uc = _inproj(xc, pre, sc_c, sh_c, w_in)
            ma, mca = _lru_branch(u, uc, conv_w[e], conv_b[e].reshape(1, -1),
                                  wg_all[e], bg_all[e], lam_all[e])
            fy = _seq_fourier(u, w12[e], dft_consts)
            x = _outproj_even(ma, fy, u, x, w_out, post, gt_l)
            if need_ctx:
                urc, uic = _chanmix(uc, w12[e])
                fyc = _seq_dft_dense(urc, uic)
                xc = _outproj_even(mca, fyc, uc, xc, w_out, post, gt_c)
        else:
            j = layer // 2
            w_in = (ret_w_in[j] * kscale).astype(BF16)
            w_out = ret_w_out[j].astype(BF16)
            u = _inproj(x, pre, sc_l, sh_l, w_in, rope=(cos_t, sin_t, 2 * qk))
            uc = _inproj(xc, pre, sc_c, sh_c, w_in, rope=(cos_c, sin_c, 2 * qk))
            m, mc = _retention(u, uc, ret_log_gamma[j])
            x = _outproj_odd(m, u, x, w_out, post, gt_l)
            if need_ctx:
                xc = _outproj_odd(mc, uc, xc, w_out, post, gt_c)
    return x
```

```python
import functools
import math

import numpy as np
import jax
import jax.numpy as jnp
from jax import lax
from jax.experimental import pallas as pl
from jax.experimental.pallas import tpu as pltpu

F32 = jnp.float32
BF16 = jnp.bfloat16

EPS = 1e-6
LOG2E = math.log2(math.e)
LRU_C = 8.0
LRU_HEADS = 8
FNET_GROUPS = 4
RET_HEADS = 4
GRID_W = 64
ROPE_THETA = 10000.0

LANES = 128
VMEM_LIMIT = 56 * 1024 * 1024

ROW_TILE = 512
LRU_CHUNK = 256
LRU_UNROLL = 4
RET_CHUNK = 256
FFT_N2 = 128
FFT_J = 16


def _cparams(n_axes):
    return pltpu.CompilerParams(dimension_semantics=("arbitrary",) * n_axes,
                                vmem_limit_bytes=VMEM_LIMIT)


def _silu(v):
    return v * jax.nn.sigmoid(v)


def _ada_kernel(cond_ref, w_ref, b_ref, o_ref):
    s = _silu(cond_ref[...])
    o_ref[...] = jnp.dot(s, w_ref[...], preferred_element_type=F32,
                         precision=lax.Precision.HIGHEST) + b_ref[...]


def _ada_mod(cond8, mod_w, mod_b):
    depth, d, n3 = mod_w.shape
    tn = d
    return pl.pallas_call(
        _ada_kernel,
        out_shape=jax.ShapeDtypeStruct((depth, 8, n3), F32),
        grid=(depth, n3 // tn),
        in_specs=[
            pl.BlockSpec((8, d), lambda l, j: (0, 0)),
            pl.BlockSpec((None, d, tn), lambda l, j: (l, 0, j)),
            pl.BlockSpec((None, 1, tn), lambda l, j: (l, 0, j)),
        ],
        out_specs=pl.BlockSpec((None, 8, tn), lambda l, j: (l, 0, j)),
        compiler_params=_cparams(2),
        name="ada_mod",
    )(cond8, mod_w, mod_b.reshape(depth, 1, n3))


def _inproj_kernel(x_ref, g_ref, sc_ref, sh_ref, w_ref, *rest, rope_cols, chunk):
    if rope_cols:
        cos_ref, sin_ref, o_ref = rest
    else:
        (o_ref,) = rest
    x = x_ref[...]
    ms = jnp.mean(x * x, axis=-1, keepdims=True)
    h = (x * lax.rsqrt(ms + EPS)) * g_ref[...]
    h = h * (1.0 + sc_ref[...]) + sh_ref[...]
    hb = h.astype(BF16)
    n_out = o_ref.shape[-1]
    for c0 in range(0, n_out, chunk):
        r = jnp.dot(hb, w_ref[:, c0:c0 + chunk], preferred_element_type=F32)
        if c0 < rope_cols:
            parts = []
            for s0 in range(0, chunk, LANES):
                col = (c0 + s0) % (2 * LANES)
                ra = r[:, s0:s0 + LANES]
                parts.append(ra * cos_ref[:, col:col + LANES]
                             + pltpu.roll(ra, LANES // 2, axis=1) * sin_ref[:, col:col + LANES])
            r = jnp.concatenate(parts, axis=1)
        o_ref[:, c0:c0 + chunk] = r.astype(o_ref.dtype)


def _inproj(x, g, scale, shift, w, rope=None):
    b, t, d = x.shape
    n = w.shape[1]
    tm = min(ROW_TILE, t)
    in_specs = [
        pl.BlockSpec((None, tm, d), lambda bi, i: (bi, i, 0)),
        pl.BlockSpec((1, d), lambda bi, i: (0, 0)),
        pl.BlockSpec((None, 1, d), lambda bi, i: (bi, 0, 0)),
        pl.BlockSpec((None, 1, d), lambda bi, i: (bi, 0, 0)),
        pl.BlockSpec((d, n), lambda bi, i: (0, 0), pipeline_mode=pl.Buffered(1)),
    ]
    args = [x, g, scale, shift, w]
    rope_cols = 0
    if rope is not None:
        cos_t, sin_t, rope_cols = rope
        in_specs += [pl.BlockSpec((tm, 2 * LANES), lambda bi, i: (i, 0)),
                     pl.BlockSpec((tm, 2 * LANES), lambda bi, i: (i, 0))]
        args += [cos_t, sin_t]
    return pl.pallas_call(
        functools.partial(_inproj_kernel, rope_cols=rope_cols, chunk=2 * LANES),
        out_shape=jax.ShapeDtypeStruct((b, t, n), BF16),
        grid=(b, t // tm),
        in_specs=in_specs,
        out_specs=pl.BlockSpec((None, tm, n), lambda bi, i: (bi, i, 0)),
        compiler_params=_cparams(2),
        name="inproj",
    )(*args)


def _group_scan(a, b, reverse):
    rows, lanes = a.shape
    ng = rows // 8
    a = a.reshape(ng, 8, lanes)
    b = b.reshape(ng, 8, lanes)
    sub = lax.broadcasted_iota(jnp.int32, (1, 8, lanes), 1)
    for dd in (1, 2, 4):
        if reverse:
            m = sub <= 7 - dd
            sh = 8 - dd
        else:
            m = sub >= dd
            sh = dd
        a_sh = pltpu.roll(a, sh, axis=1)
        b_sh = pltpu.roll(b, sh, axis=1)
        b = b + a * jnp.where(m, b_sh, 0.0)
        a = a * jnp.where(m, a_sh, 1.0)
    return a.reshape(rows, lanes), b.reshape(rows, lanes)


def _carry_chain(a, b, carry, reverse):
    ng = a.shape[0] // 8
    outs = [None] * ng
    order = range(ng - 1, -1, -1) if reverse else range(ng)
    for gi in order:
        hg = b[8 * gi:8 * gi + 8] + a[8 * gi:8 * gi + 8] * carry
        carry = hg[0:1] if reverse else hg[7:8]
        outs[gi] = hg
    return jnp.concatenate(outs, axis=0), carry


def _scan_chunk(a, b, carry, reverse):
    a, b = _group_scan(a, b, reverse)
    return _carry_chain(a, b, carry, reverse)


def _sqrt_nonneg(v):
    return jnp.where(v > 0.0, v * lax.rsqrt(v), 0.0)


def _lru_kernel(xa_ref, ga_ref, xca_ref, gca_ref, cw_ref, cb_ref, wg_ref, bg_ref, lam_ref,
                o_ref, oc_ref, zs_ref, hb_ref):
    cw = cw_ref[...]
    cb = cb_ref[...]
    nsp2 = []
    for d in (0, 1):
        y = -lam_ref[d]
        nsp2.append((-LRU_C * LOG2E) * (jnp.maximum(y, 0.0) + jnp.log1p(jnp.exp(-jnp.abs(y)))))

    def run_seq(x_ref, g_ref, out_ref, carry_f, carry_r):
        tn = x_ref.shape[0]
        rn = min(LRU_CHUNK, tn)
        nch = tn // rn
        assert nch == 1 or nch % 2 == 0

        def conv_body(c, carry):
            r0 = pl.multiple_of(c * rn, rn)
            p0 = pl.multiple_of(jnp.maximum(r0 - 16, 0), 16)
            n0 = pl.multiple_of(jnp.minimum(r0 + rn, tn - 16), 16)
            cur = x_ref[pl.ds(r0, rn), :].astype(F32)
            prev = jnp.where(c > 0, x_ref[pl.ds(p0, 16), :].astype(F32), 0.0)
            nxt = jnp.where(c < nch - 1, x_ref[pl.ds(n0, 16), :].astype(F32), 0.0)
            ext = jnp.concatenate([prev, cur, nxt], axis=0)
            z = cb + cw[0:1] * pltpu.roll(ext, 2, axis=0)[16:16 + rn]
            z = z + cw[1:2] * pltpu.roll(ext, 1, axis=0)[16:16 + rn]
            z = z + cw[2:3] * cur
            z = z + cw[3:4] * pltpu.roll(ext, rn + 31, axis=0)[16:16 + rn]
            zs_ref[pl.ds(r0, rn), :] = z
            return carry

        lax.fori_loop(0, nch, conv_body, 0)

        def coeffs(r0, d):
            z = zs_ref[pl.ds(r0, rn), :]
            e = jnp.exp2(jnp.dot(z.astype(BF16), wg_ref[d], preferred_element_type=F32) + bg_ref[d])
            r = 1.0 / (1.0 + e[:, :LANES])
            i = 1.0 / (1.0 + e[:, LANES:])
            a = jnp.exp2(nsp2[d] * r)
            return a, _sqrt_nonneg(1.0 - a * a) * (i * z)

        def gate(r0):
            return _silu(g_ref[pl.ds(r0, rn), :].astype(F32))

        if nch == 1:
            a, b = coeffs(0, 0)
            hf, carry_f = _scan_chunk(a, b, carry_f, False)
            a, b = coeffs(0, 1)
            hr, carry_r = _scan_chunk(a, b, carry_r, True)
            out_ref[...] = ((hf + hr) * gate(0)).astype(out_ref.dtype)
            return carry_f, carry_r

        def step(c, carry, combine):
            cf, cr = carry
            r0 = pl.multiple_of(c * rn, rn)
            r1 = pl.multiple_of((nch - 1 - c) * rn, rn)
            a, b = coeffs(r0, 0)
            hf, cf = _scan_chunk(a, b, cf, False)
            a, b = coeffs(r1, 1)
            hr, cr = _scan_chunk(a, b, cr, True)
            if combine:
                out_ref[pl.ds(r0, rn), :] = ((hf + hb_ref[pl.ds(r0, rn), :]) * gate(r0)).astype(out_ref.dtype)
                out_ref[pl.ds(r1, rn), :] = ((hb_ref[pl.ds(r1, rn), :] + hr) * gate(r1)).astype(out_ref.dtype)
            else:
                hb_ref[pl.ds(r0, rn), :] = hf
                hb_ref[pl.ds(r1, rn), :] = hr
            return cf, cr

        carry = lax.fori_loop(0, nch // 2, functools.partial(step, combine=False), (carry_f, carry_r),
                              unroll=LRU_UNROLL)
        return lax.fori_loop(nch // 2, nch, functools.partial(step, combine=True), carry,
                             unroll=LRU_UNROLL)

    zero = jnp.zeros((1, LANES), F32)
    cf, cr = run_seq(xca_ref, gca_ref, oc_ref, zero, zero)
    run_seq(xa_ref, ga_ref, o_ref, cf, cr)


def _lru_branch(u, uc, conv_w, conv_b, wg, bg, lam):
    b, t, _ = u.shape
    tc = uc.shape[1]
    w = conv_w.shape[1]
    nh = w // LANES
    ga0 = 2 * nh
    return pl.pallas_call(
        _lru_kernel,
        out_shape=(jax.ShapeDtypeStruct((b, t, w), BF16), jax.ShapeDtypeStruct((b, tc, w), BF16)),
        grid=(b, nh),
        in_specs=[
            pl.BlockSpec((None, t, LANES), lambda bi, h: (bi, 0, h)),
            pl.BlockSpec((None, t, LANES), lambda bi, h: (bi, 0, ga0 + h)),
            pl.BlockSpec((None, tc, LANES), lambda bi, h: (bi, 0, h)),
            pl.BlockSpec((None, tc, LANES), lambda bi, h: (bi, 0, ga0 + h)),
            pl.BlockSpec((conv_w.shape[0], LANES), lambda bi, h: (0, h)),
            pl.BlockSpec((1, LANES), lambda bi, h: (0, h)),
            pl.BlockSpec((2, None, LANES, 2 * LANES), lambda bi, h: (0, h, 0, 0)),
            pl.BlockSpec((2, None, 1, 2 * LANES), lambda bi, h: (0, h, 0, 0)),
            pl.BlockSpec((2, None, 1, LANES), lambda bi, h: (0, h, 0, 0)),
        ],
        out_specs=(pl.BlockSpec((None, t, LANES), lambda bi, h: (bi, 0, h)),
                   pl.BlockSpec((None, tc, LANES), lambda bi, h: (bi, 0, h))),
        scratch_shapes=[pltpu.VMEM((t, LANES), F32), pltpu.VMEM((t, LANES), F32)],
        compiler_params=_cparams(2),
        name="rglru",
    )(u, u, uc, uc, conv_w, conv_b, wg, bg, lam)


def _w12_kernel(cs_ref, wf_ref, o_ref):
    gd = wf_ref.shape[0]
    for half in range(gd // LANES):
        wf = wf_ref[:, half * LANES:(half + 1) * LANES]
        o_ref[half, :, :LANES] = jnp.dot(cs_ref[:gd, :], wf, preferred_element_type=F32,
                                         precision=lax.Precision.HIGHEST).astype(o_ref.dtype)
        o_ref[half, :, LANES:] = jnp.dot(cs_ref[gd:, :], wf, preferred_element_type=F32,
                                         precision=lax.Precision.HIGHEST).astype(o_ref.dtype)


def _fourier_weights(fnet_w):
    ne, ng, gd, _ = fnet_w.shape
    nhalf = gd // LANES
    idx = np.arange(gd)
    ang = 2.0 * np.pi * ((idx[:, None] * idx[None, :]) % gd) / gd
    cs = np.concatenate([np.cos(ang), np.sin(ang)], axis=0) / math.sqrt(gd)
    out = pl.pallas_call(
        _w12_kernel,
        out_shape=jax.ShapeDtypeStruct((ne, ng, nhalf, gd, 2 * LANES), BF16),
        grid=(ne, ng),
        in_specs=[pl.BlockSpec((2 * gd, gd), lambda e, g: (0, 0)),
                  pl.BlockSpec((None, None, gd, gd), lambda e, g: (e, g, 0, 0))],
        out_specs=pl.BlockSpec((None, None, nhalf, gd, 2 * LANES), lambda e, g: (e, g, 0, 0, 0)),
        compiler_params=_cparams(2),
        name="fourier_weights",
    )(jnp.asarray(cs, F32), fnet_w)
    return out.reshape(ne, ng * nhalf, gd, 2 * LANES)


def _chanmix_kernel(xb_ref, w_ref, ur_ref, ui_ref):
    ncb, gd, _ = w_ref.shape
    per_group = gd // LANES
    for cb in range(ncb):
        g = cb // per_group
        r = jnp.dot(xb_ref[:, g * gd:(g + 1) * gd], w_ref[cb], preferred_element_type=F32)
        ur_ref[:, cb * LANES:(cb + 1) * LANES] = r[:, :LANES]
        ui_ref[:, cb * LANES:(cb + 1) * LANES] = r[:, LANES:]


def _chanmix(u, w12):
    b, t, _ = u.shape
    ncb, gd, _ = w12.shape
    w = ncb * LANES
    tm = min(ROW_TILE, t)
    out = jax.ShapeDtypeStruct((b, t, w), F32)
    return pl.pallas_call(
        _chanmix_kernel,
        out_shape=(out, out),
        grid=(b, t // tm),
        in_specs=[pl.BlockSpec((None, tm, w), lambda bi, i: (bi, i, 1)),
                  pl.BlockSpec((ncb, gd, 2 * LANES), lambda bi, i: (0, 0, 0))],
        out_specs=(pl.BlockSpec((None, tm, w), lambda bi, i: (bi, i, 0)),
                   pl.BlockSpec((None, tm, w), lambda bi, i: (bi, i, 0))),
        compiler_params=_cparams(2),
        name="fourier_chanmix",
    )(u, w12)


def _dft_a_kernel(xb_ref, w_ref, ea_ref, twr_ref, twi_ref, br_ref, bi_ref, ur_s, ui_s):
    n2, nj, gd = xb_ref.shape
    u = jnp.dot(xb_ref[...].reshape(n2 * nj, gd), w_ref[...], preferred_element_type=F32)
    ur_s[...] = u[:, :LANES]
    ui_s[...] = u[:, LANES:]
    xr = jnp.concatenate([ur_s[pl.ds(j, n2, stride=nj), :] for j in range(nj)], axis=1)
    xi = jnp.concatenate([ui_s[pl.ds(j, n2, stride=nj), :] for j in range(nj)], axis=1)
    rhs = jnp.concatenate([xr, xi], axis=0).astype(BF16)
    res = jnp.dot(ea_ref[...], rhs, preferred_element_type=F32)
    for j in range(nj):
        pr = res[:n2, j * LANES:(j + 1) * LANES]
        pi = res[n2:, j * LANES:(j + 1) * LANES]
        tr = twr_ref[j]
        ti = twi_ref[j]
        br_ref[j] = (pr * tr - pi * ti).astype(br_ref.dtype)
        bi_ref[j] = (pr * ti + pi * tr).astype(bi_ref.dtype)


def _dft_b_kernel(br_ref, bi_ref, gb_ref, eb_ref, o_ref, sr_ref, si_ref):
    n1, nj, _ = br_ref.shape
    sr_ref[...] = br_ref[...].astype(F32).reshape(n1 * nj, LANES)
    si_ref[...] = bi_ref[...].astype(F32).reshape(n1 * nj, LANES)
    xr = jnp.concatenate([sr_ref[pl.ds(j, n1, stride=nj), :] for j in range(nj)], axis=1)
    xi = jnp.concatenate([si_ref[pl.ds(j, n1, stride=nj), :] for j in range(nj)], axis=1)
    rhs = jnp.concatenate([xr, xi], axis=0).astype(BF16)
    res = jnp.dot(eb_ref[...], rhs, preferred_element_type=F32)
    for j in range(nj):
        sr_ref[pl.ds(j, n1, stride=nj), :] = res[:, j * LANES:(j + 1) * LANES]
    fy = sr_ref[...].reshape(n1, nj, LANES)
    o_ref[...] = (fy * _silu(gb_ref[...].astype(F32))).astype(o_ref.dtype)


def _dft_consts(t):
    n2 = FFT_N2
    n1 = t // n2
    i2 = np.arange(n2)
    a2 = 2.0 * np.pi * ((i2[:, None] * i2[None, :]) % n2) / n2
    er, ei = np.cos(a2), np.sin(a2)
    ea = np.block([[er, -ei], [ei, er]])
    i1 = np.arange(n1)
    atw = 2.0 * np.pi * ((i1[:, None] * i2[None, :]) % t) / t
    a1 = 2.0 * np.pi * ((i1[:, None] * i1[None, :]) % n1) / n1
    eb = np.concatenate([np.cos(a1), -np.sin(a1)], axis=1) / math.sqrt(t)
    twr = jnp.broadcast_to(jnp.asarray(np.cos(atw), F32)[:, :, None], (n1, n2, LANES))
    twi = jnp.broadcast_to(jnp.asarray(np.sin(atw), F32)[:, :, None], (n1, n2, LANES))
    return jnp.asarray(ea, F32).astype(BF16), twr, twi, jnp.asarray(eb, F32).astype(BF16)


def _seq_fourier(u, w12, consts):
    b, t, n_in = u.shape
    ncb, gd, _ = w12.shape
    w = ncb * LANES
    n2 = FFT_N2
    n1 = t // n2
    nj = FFT_J
    ea, twr, twi, eb = consts
    u4 = u.reshape(b, n2, n1, n_in)
    xb0 = w // gd
    per_group = gd // LANES
    mid = jax.ShapeDtypeStruct((b, n1, n2, w), BF16)
    br, bi = pl.pallas_call(
        _dft_a_kernel,
        out_shape=(mid, mid),
        grid=(n1 // nj, b, ncb),
        in_specs=[
            pl.BlockSpec((None, n2, nj, gd), lambda tb, bi_, cb: (bi_, 0, tb, xb0 + cb // per_group)),
            pl.BlockSpec((None, gd, 2 * LANES), lambda tb, bi_, cb: (cb, 0, 0)),
            pl.BlockSpec((2 * n2, 2 * n2), lambda tb, bi_, cb: (0, 0)),
            pl.BlockSpec((nj, n2, LANES), lambda tb, bi_, cb: (tb, 0, 0)),
            pl.BlockSpec((nj, n2, LANES), lambda tb, bi_, cb: (tb, 0, 0)),
        ],
        out_specs=(pl.BlockSpec((None, nj, n2, LANES), lambda tb, bi_, cb: (bi_, tb, 0, cb)),
                   pl.BlockSpec((None, nj, n2, LANES), lambda tb, bi_, cb: (bi_, tb, 0, cb))),
        scratch_shapes=[pltpu.VMEM((n2 * nj, LANES), F32), pltpu.VMEM((n2 * nj, LANES), F32)],
        compiler_params=_cparams(3),
        name="dft_stage_a",
    )(u4, w12, ea, twr, twi)
    gb0 = (n_in - w) // LANES
    out = pl.pallas_call(
        _dft_b_kernel,
        out_shape=jax.ShapeDtypeStruct((b, n1, n2, w), BF16),
        grid=(b, n2 // nj, w // LANES),
        in_specs=[
            pl.BlockSpec((None, n1, nj, LANES), lambda bi_, kb, cb: (bi_, 0, kb, cb)),
            pl.BlockSpec((None, n1, nj, LANES), lambda bi_, kb, cb: (bi_, 0, kb, cb)),
            pl.BlockSpec((None, n1, nj, LANES), lambda bi_, kb, cb: (bi_, 0, kb, gb0 + cb)),
            pl.BlockSpec((n1, 2 * n1), lambda bi_, kb, cb: (0, 0)),
        ],
        out_specs=pl.BlockSpec((None, n1, nj, LANES), lambda bi_, kb, cb: (bi_, 0, kb, cb)),
        scratch_shapes=[pltpu.VMEM((n1 * nj, LANES), F32), pltpu.VMEM((n1 * nj, LANES), F32)],
        compiler_params=_cparams(3),
        name="dft_stage_b",
    )(br, bi, u.reshape(b, n1, n2, n_in), eb)
    return out.reshape(b, t, w)


def _dft_dense_kernel(ur_ref, ui_ref, gb_ref, ec_ref, o_ref):
    rhs = jnp.concatenate([ur_ref[...], ui_ref[...]], axis=0).astype(BF16)
    fy = jnp.dot(ec_ref[...], rhs, preferred_element_type=F32)
    o_ref[...] = (fy * _silu(gb_ref[...].astype(F32))).astype(o_ref.dtype)


def _seq_dft_dense(ur, ui, u):
    b, t, w = ur.shape
    gb_blk = u.shape[-1] // w - 1
    it = np.arange(t)
    ang = 2.0 * np.pi * ((it[:, None] * it[None, :]) % t) / t
    ec = jnp.asarray(np.concatenate([np.cos(ang), -np.sin(ang)], axis=1) / math.sqrt(t), F32).astype(BF16)
    return pl.pallas_call(
        _dft_dense_kernel,
        out_shape=jax.ShapeDtypeStruct((b, t, w), BF16),
        grid=(b,),
        in_specs=[pl.BlockSpec((None, t, w), lambda bi: (bi, 0, 0)),
                  pl.BlockSpec((None, t, w), lambda bi: (bi, 0, 0)),
                  pl.BlockSpec((None, t, w), lambda bi: (bi, 0, gb_blk)),
                  pl.BlockSpec((t, 2 * t), lambda bi: (0, 0))],
        out_specs=pl.BlockSpec((None, t, w), lambda bi: (bi, 0, 0)),
        compiler_params=_cparams(1),
        name="dft_dense",
    )(ur, ui, u, ec)


def _post(y, x_ref, pg_ref, gate_ref, o_ref):
    ms = jnp.mean(y * y, axis=-1, keepdims=True)
    yn = (y * lax.rsqrt(ms + EPS)) * pg_ref[...]
    o_ref[...] = x_ref[...] + gate_ref[...] * yn


def _outproj_even_kernel(ma_ref, mb_ref, x_ref, w_ref, pg_ref, gate_ref, o_ref):
    wa = ma_ref.shape[-1]
    y = jnp.dot(ma_ref[...], w_ref[:wa, :], preferred_element_type=F32)
    y = y + jnp.dot(mb_ref[...], w_ref[wa:, :], preferred_element_type=F32)
    _post(y, x_ref, pg_ref, gate_ref, o_ref)


def _outproj_odd_kernel(on_ref, g_ref, x_ref, w_ref, pg_ref, gate_ref, o_ref):
    m = (on_ref[...].astype(F32) * _silu(g_ref[...].astype(F32))).astype(BF16)
    y = jnp.dot(m, w_ref[...], preferred_element_type=F32)
    _post(y, x_ref, pg_ref, gate_ref, o_ref)


def _outproj_even(ma, mb, x, w_out, pg, gate):
    b, t, d = x.shape
    wa = ma.shape[-1]
    wb = mb.shape[-1]
    tm = min(ROW_TILE, t)
    row = lambda bi, i: (bi, i, 0)
    return pl.pallas_call(
        _outproj_even_kernel,
        out_shape=jax.ShapeDtypeStruct((b, t, d), F32),
        grid=(b, t // tm),
        in_specs=[
            pl.BlockSpec((None, tm, wa), row),
            pl.BlockSpec((None, tm, wb), row),
            pl.BlockSpec((None, tm, d), row),
            pl.BlockSpec((wa + wb, d), lambda bi, i: (0, 0)),
            pl.BlockSpec((1, d), lambda bi, i: (0, 0)),
            pl.BlockSpec((None, 1, d), lambda bi, i: (bi, 0, 0)),
        ],
        out_specs=pl.BlockSpec((None, tm, d), row),
        compiler_params=_cparams(2),
        name="outproj_even",
    )(ma, mb, x, w_out, pg, gate)


def _outproj_odd(on, u, x, w_out, pg, gate):
    b, t, d = x.shape
    k = on.shape[-1]
    g_blk = u.shape[-1] // k - 1
    tm = min(ROW_TILE, t)
    row = lambda bi, i: (bi, i, 0)
    return pl.pallas_call(
        _outproj_odd_kernel,
        out_shape=jax.ShapeDtypeStruct((b, t, d), F32),
        grid=(b, t // tm),
        in_specs=[
            pl.BlockSpec((None, tm, k), row),
            pl.BlockSpec((None, tm, k), lambda bi, i: (bi, i, g_blk)),
            pl.BlockSpec((None, tm, d), row),
            pl.BlockSpec((k, d), lambda bi, i: (0, 0)),
            pl.BlockSpec((1, d), lambda bi, i: (0, 0)),
            pl.BlockSpec((None, 1, d), lambda bi, i: (bi, 0, 0)),
        ],
        out_specs=pl.BlockSpec((None, tm, d), row),
        compiler_params=_cparams(2),
        name="outproj_odd",
    )(on, u, x, w_out, pg, gate)


def _ret_kernel(lg_ref, q_ref, k_ref, v_ref, *rest, reverse, final):
    if final:
        op_ref, s0_ref, o_ref, se_ref, s_ref, dm_ref, qd_ref, kd_ref = rest
    else:
        s0_ref, o_ref, se_ref, s_ref, dm_ref, qd_ref, kd_ref = rest
    nh, dk, dv = s_ref.shape
    cr = q_ref.shape[0]
    d = 1 if reverse else 0

    @pl.when(pl.program_id(1) == 0)
    def _init():
        s_ref[...] = s0_ref[...]
        ii = lax.broadcasted_iota(jnp.int32, (cr, cr), 0).astype(F32)
        jj = lax.broadcasted_iota(jnp.int32, (cr, cr), 1).astype(F32)
        diff = jj - ii
        rq = lax.broadcasted_iota(jnp.int32, (cr, dv), 0).astype(F32)
        rk = lax.broadcasted_iota(jnp.int32, (cr, dk), 0).astype(F32)
        for h in range(nh):
            lgh = lg_ref[d, h]
            if reverse:
                mask = diff > 0
                expo = diff
                qe = cr - rq
                ke = rk
            else:
                mask = diff <= 0
                expo = -diff
                qe = rq + 1.0
                ke = (cr - 1.0) - rk
            dm_ref[h] = jnp.where(mask, jnp.exp(lgh * jnp.where(mask, expo, 0.0)), 0.0)
            qd_ref[h] = jnp.exp(lgh * qe)
            kd_ref[h] = jnp.exp(lgh * ke)

    for h in range(nh):
        cdec = jnp.exp(jnp.full((1, dv), lg_ref[d, h] * cr, F32))
        qh = q_ref[:, h * dk:(h + 1) * dk]
        kh = k_ref[:, h * dk:(h + 1) * dk]
        vh = v_ref[:, h * dv:(h + 1) * dv]
        s = lax.dot_general(qh, kh, (((1,), (1,)), ((), ())), preferred_element_type=F32)
        o = jnp.dot((s * dm_ref[h]).astype(BF16), vh, preferred_element_type=F32)
        st = s_ref[h]
        o = o + qd_ref[h] * jnp.dot(qh, st.astype(BF16), preferred_element_type=F32)
        kdec = (kh.astype(F32) * kd_ref[h]).astype(BF16)
        s_ref[h] = cdec * st + lax.dot_general(kdec, vh, (((0,), (0,)), ((), ())),
                                               preferred_element_type=F32)
        if final:
            o = o + op_ref[:, h * dv:(h + 1) * dv].astype(F32)
            mu = jnp.mean(o, axis=-1, keepdims=True)
            oc = o - mu
            var = jnp.mean(oc * oc, axis=-1, keepdims=True)
            o = oc * lax.rsqrt(var + EPS)
        o_ref[:, h * dv:(h + 1) * dv] = o.astype(o_ref.dtype)

    @pl.when(pl.program_id(1) == pl.num_programs(1) - 1)
    def _fin():
        se_ref[...] = s_ref[...]


def _ret_pass(u, lg, s0, o_prev, reverse, final):
    b, t, _ = u.shape
    _, nh, dk, dv = s0.shape
    cr = min(RET_CHUNK, t)
    nc = t // cr
    qk = nh * dk
    vw = nh * dv
    if reverse:
        ci = lambda i: nc - 1 - i
    else:
        ci = lambda i: i
    in_specs = [
        pl.BlockSpec(memory_space=pltpu.SMEM),
        pl.BlockSpec((None, cr, qk), lambda bi, i: (bi, ci(i), 0)),
        pl.BlockSpec((None, cr, qk), lambda bi, i: (bi, ci(i), 1)),
        pl.BlockSpec((None, cr, vw), lambda bi, i: (bi, ci(i), (2 * qk) // vw)),
    ]
    args = [lg, u, u, u]
    if final:
        in_specs.append(pl.BlockSpec((None, cr, vw), lambda bi, i: (bi, ci(i), 0)))
        args.append(o_prev)
    in_specs.append(pl.BlockSpec((None, nh, dk, dv), lambda bi, i: (bi, 0, 0, 0)))
    args.append(s0)
    return pl.pallas_call(
        functools.partial(_ret_kernel, reverse=reverse, final=final),
        out_shape=(jax.ShapeDtypeStruct((b, t, vw), BF16),
                   jax.ShapeDtypeStruct((b, nh, dk, dv), F32)),
        grid=(b, nc),
        in_specs=in_specs,
        out_specs=(pl.BlockSpec((None, cr, vw), lambda bi, i: (bi, ci(i), 0)),
                   pl.BlockSpec((None, nh, dk, dv), lambda bi, i: (bi, 0, 0, 0))),
        scratch_shapes=[
            pltpu.VMEM((nh, dk, dv), F32),
            pltpu.VMEM((nh, cr, cr), F32),
            pltpu.VMEM((nh, cr, dv), F32),
            pltpu.VMEM((nh, cr, dk), F32),
        ],
        compiler_params=_cparams(2),
        name="retention_rev" if reverse else "retention_fwd",
    )(*args)


def _retention(u, uc, lg):
    b = u.shape[0]
    nh = RET_HEADS
    dk = u.shape[-1] // (6 * nh)
    zero = jnp.zeros((b, nh, dk, 2 * dk), F32)
    oc_r, s_r = _ret_pass(uc, lg, zero, None, True, False)
    mc, s_f = _ret_pass(uc, lg, zero, oc_r, False, True)
    o_r, _ = _ret_pass(u, lg, s_r, None, True, False)
    m, _ = _ret_pass(u, lg, s_f, o_r, False, True)
    return m, mc


def _rope_tables(t):
    nf = LANES // 2
    rows = t // GRID_W
    inv = ROPE_THETA ** (-jnp.arange(nf, dtype=F32) / nf)
    ar = jnp.arange(rows, dtype=F32)[:, None] * inv
    ac = jnp.arange(GRID_W, dtype=F32)[:, None] * inv
    exp_r = lambda v: jnp.repeat(v, GRID_W, axis=0)
    exp_c = lambda v: jnp.tile(v, (rows, 1))
    cr, sr, cc, sc = exp_r(jnp.cos(ar)), exp_r(jnp.sin(ar)), exp_c(jnp.cos(ac)), exp_c(jnp.sin(ac))
    cos_t = jnp.concatenate([cr, cr, cc, cc], axis=1)
    sin_t = jnp.concatenate([-sr, sr, -sc, sc], axis=1)
    return cos_t, sin_t


def kernel(x, c, ctx, c_ctx, mod_w, mod_b, pre_g, post_g, mix_w_in, mix_w_out, conv_w, conv_b,
           lru_wa, lru_ba, lru_wx, lru_bx, lru_lam, fnet_w, ret_w_in, ret_w_out, ret_log_gamma):
    b, t, d = x.shape
    tc = ctx.shape[1]
    depth = mod_w.shape[0]
    assert t % ROW_TILE == 0 and t % (FFT_N2 * FFT_J) == 0 and (t // FFT_N2) % FFT_J == 0
    assert tc % 16 == 0 and tc <= ROW_TILE and b + 1 <= 8

    cond8 = jnp.zeros((8, d), F32).at[:b].set(c).at[b].set(c_ctx)
    mods = _ada_mod(cond8, mod_w, mod_b)

    w12 = _fourier_weights(fnet_w)
    dft_consts = _dft_consts(t)
    cos_t, sin_t = _rope_tables(t)
    cos_c = jnp.ones((tc, 2 * LANES), F32)
    sin_c = jnp.zeros((tc, 2 * LANES), F32)

    nh_l = LRU_HEADS
    hd = lru_wa.shape[-1]
    wg_all = (jnp.concatenate([lru_wa, lru_wx], axis=-1) * (-LOG2E)).astype(BF16)
    bg_all = jnp.concatenate([lru_ba.reshape(-1, 2, nh_l, 1, hd),
                              lru_bx.reshape(-1, 2, nh_l, 1, hd)], axis=-1) * (-LOG2E)
    lam_all = lru_lam.reshape(-1, 2, nh_l, 1, hd)

    qk = ret_w_in.shape[-1] // 6
    kscale = jnp.concatenate([jnp.ones((qk,), F32), jnp.full((qk,), (qk // RET_HEADS) ** -0.5, F32),
                              jnp.ones((4 * qk,), F32)])

    xc = ctx
    for layer in range(depth):
        need_ctx = layer < depth - 1
        m_l = mods[layer]
        shift, scale, gate = m_l[:, :d], m_l[:, d:2 * d], m_l[:, 2 * d:]
        sc_l, sh_l, gt_l = (v[:b].reshape(b, 1, d) for v in (scale, shift, gate))
        sc_c, sh_c, gt_c = (jnp.broadcast_to(v[b].reshape(1, 1, d), (b, 1, d)) for v in (scale, shift, gate))
        pre = pre_g[layer].reshape(1, d)
        post = post_g[layer].reshape(1, d)
        if layer % 2 == 0:
            e = layer // 2
            w_in = mix_w_in[e].astype(BF16)
            w_out = mix_w_out[e].astype(BF16)
            u = _inproj(x, pre, sc_l, sh_l, w_in)
            uc = _inproj(xc, pre, sc_c, sh_c, w_in)
            ma, mca = _lru_branch(u, uc, conv_w[e], conv_b[e].reshape(1, -1),
                                  wg_all[e], bg_all[e], lam_all[e])
            mb = _seq_fourier(u, w12[e], dft_consts)
            x = _outproj_even(ma, mb, x, w_out, post, gt_l)
            if need_ctx:
                urc, uic = _chanmix(uc, w12[e])
                mcb = _seq_dft_dense(urc, uic, uc)
                xc = _outproj_even(mca, mcb, xc, w_out, post, gt_c)
        else:
            j = layer // 2
            w_in = (ret_w_in[j] * kscale).astype(BF16)
            w_out = ret_w_out[j].astype(BF16)
            u = _inproj(x, pre, sc_l, sh_l, w_in, rope=(cos_t, sin_t, 2 * qk))
            uc = _inproj(xc, pre, sc_c, sh_c, w_in, rope=(cos_c, sin_c, 2 * qk))
            m, mc = _retention(u, uc, ret_log_gamma[j])
            x = _outproj_odd(m, u, x, w_out, post, gt_l)
            if need_ctx:
                xc = _outproj_odd(mc, uc, xc, w_out, post, gt_c)
    return x
```

```python
import functools
import math

import numpy as np
import jax
import jax.numpy as jnp
from jax import lax
from jax.experimental import pallas as pl
from jax.experimental.pallas import tpu as pltpu

F32 = jnp.float32
BF16 = jnp.bfloat16

EPS = 1e-6
LOG2E = math.log2(math.e)
LRU_C = 8.0
LRU_HEADS = 8
FNET_GROUPS = 4
RET_HEADS = 4
GRID_W = 64
ROPE_THETA = 10000.0

LANES = 128
VMEM_LIMIT = 56 * 1024 * 1024

ROW_TILE = 512
LRU_CHUNK = 256
LRU_UNROLL = 4
RET_CHUNK = 256
FFT_N2 = 128
FFT_J = 16


def _cparams(n_axes):
    return pltpu.CompilerParams(dimension_semantics=("arbitrary",) * n_axes,
                                vmem_limit_bytes=VMEM_LIMIT)


def _silu(v):
    return v * jax.nn.sigmoid(v)


def _ada_kernel(cond_ref, w_ref, b_ref, o_ref):
    s = _silu(cond_ref[...])
    o_ref[...] = jnp.dot(s, w_ref[...], preferred_element_type=F32,
                         precision=lax.Precision.HIGHEST) + b_ref[...]


def _ada_mod(cond8, mod_w, mod_b):
    depth, d, n3 = mod_w.shape
    tn = d
    return pl.pallas_call(
        _ada_kernel,
        out_shape=jax.ShapeDtypeStruct((depth, 8, n3), F32),
        grid=(depth, n3 // tn),
        in_specs=[
            pl.BlockSpec((8, d), lambda l, j: (0, 0)),
            pl.BlockSpec((None, d, tn), lambda l, j: (l, 0, j)),
            pl.BlockSpec((None, 1, tn), lambda l, j: (l, 0, j)),
        ],
        out_specs=pl.BlockSpec((None, 8, tn), lambda l, j: (l, 0, j)),
        compiler_params=_cparams(2),
        name="ada_mod",
    )(cond8, mod_w, mod_b.reshape(depth, 1, n3))


def _proj_kernel(*refs, out_kind, has_in, rope_cols, chunk):
    it = iter(refs)
    if out_kind:
        a_ref, b_ref = next(it), next(it)
    x_ref = next(it)
    if out_kind:
        wo_ref, pg_ref, gate_ref = next(it), next(it), next(it)
    if has_in:
        g_ref, sc_ref, sh_ref, wi_ref = next(it), next(it), next(it), next(it)
        if rope_cols:
            crow_ref, srow_ref, ccol_ref, scol_ref = next(it), next(it), next(it), next(it)
    if out_kind:
        xo_ref = next(it)
    if has_in:
        u_ref = next(it)

    x = x_ref[...]
    if out_kind == "even":
        wa = a_ref.shape[-1]
        y = jnp.dot(a_ref[...], wo_ref[:wa, :], preferred_element_type=F32)
        y = y + jnp.dot(b_ref[...], wo_ref[wa:, :], preferred_element_type=F32)
    elif out_kind == "odd":
        m = (a_ref[...].astype(F32) * _silu(b_ref[...].astype(F32))).astype(BF16)
        y = jnp.dot(m, wo_ref[...], preferred_element_type=F32)
    if out_kind:
        ms = jnp.mean(y * y, axis=-1, keepdims=True)
        x = x + gate_ref[...] * ((y * lax.rsqrt(ms + EPS)) * pg_ref[...])
        xo_ref[...] = x
    if not has_in:
        return

    ms = jnp.mean(x * x, axis=-1, keepdims=True)
    h = (x * lax.rsqrt(ms + EPS)) * g_ref[...]
    h = h * (1.0 + sc_ref[...]) + sh_ref[...]
    hb = h.astype(BF16)
    if rope_cols:
        nrow = crow_ref.shape[0]
        expand = lambda r: jnp.concatenate(
            [jnp.broadcast_to(r[g:g + 1, :], (GRID_W, LANES)) for g in range(nrow)], axis=0)
        repeat = lambda r: jnp.concatenate([r[...]] * nrow, axis=0)
        cos_t = (expand(crow_ref), repeat(ccol_ref))
        sin_t = (expand(srow_ref), repeat(scol_ref))
    n_out = u_ref.shape[-1]
    for c0 in range(0, n_out, chunk):
        r = jnp.dot(hb, wi_ref[:, c0:c0 + chunk], preferred_element_type=F32)
        if c0 < rope_cols:
            parts = []
            for s0 in range(0, chunk, LANES):
                axis = ((c0 + s0) // LANES) % 2
                ra = r[:, s0:s0 + LANES]
                parts.append(ra * cos_t[axis] + pltpu.roll(ra, LANES // 2, axis=1) * sin_t[axis])
            r = jnp.concatenate(parts, axis=1)
        u_ref[:, c0:c0 + chunk] = r.astype(u_ref.dtype)


def _proj(x, out=None, inp=None):
    b, t, d = x.shape
    tm = min(ROW_TILE, t)
    row = lambda bi, i: (bi, i, 0)
    const2 = lambda bi, i: (0, 0)
    per_b = lambda bi, i: (bi, 0, 0)
    in_specs, args, out_shape, out_specs = [], [], [], []
    out_kind = None
    if out is not None:
        out_kind, a, bb, w_out, pg, gate = out
        ka = a.shape[-1]
        in_specs.append(pl.BlockSpec((None, tm, ka), row))
        if out_kind == "even":
            in_specs.append(pl.BlockSpec((None, tm, bb.shape[-1]), row))
        else:
            g_blk = bb.shape[-1] // ka - 1
            in_specs.append(pl.BlockSpec((None, tm, ka), lambda bi, i: (bi, i, g_blk)))
        args += [a, bb]
    in_specs.append(pl.BlockSpec((None, tm, d), row))
    args.append(x)
    if out is not None:
        in_specs += [pl.BlockSpec(w_out.shape, const2, pipeline_mode=pl.Buffered(1)),
                     pl.BlockSpec((1, d), const2), pl.BlockSpec((None, 1, d), per_b)]
        args += [w_out, pg, gate]
        out_shape.append(jax.ShapeDtypeStruct((b, t, d), F32))
        out_specs.append(pl.BlockSpec((None, tm, d), row))
    rope_cols = 0
    if inp is not None:
        g, scale, shift, w_in, rope = inp
        n = w_in.shape[1]
        in_specs += [pl.BlockSpec((1, d), const2), pl.BlockSpec((None, 1, d), per_b),
                     pl.BlockSpec((None, 1, d), per_b),
                     pl.BlockSpec((d, n), const2, pipeline_mode=pl.Buffered(1))]
        args += [g, scale, shift, w_in]
        if rope is not None:
            crow, srow, ccol, scol, rope_cols = rope
            assert tm % GRID_W == 0
            nrow = tm // GRID_W
            in_specs += [pl.BlockSpec((nrow, LANES), lambda bi, i: (i, 0)),
                         pl.BlockSpec((nrow, LANES), lambda bi, i: (i, 0)),
                         pl.BlockSpec((GRID_W, LANES), const2), pl.BlockSpec((GRID_W, LANES), const2)]
            args += [crow, srow, ccol, scol]
        out_shape.append(jax.ShapeDtypeStruct((b, t, n), BF16))
        out_specs.append(pl.BlockSpec((None, tm, n), row))
    res = pl.pallas_call(
        functools.partial(_proj_kernel, out_kind=out_kind, has_in=inp is not None,
                          rope_cols=rope_cols, chunk=2 * LANES),
        out_shape=tuple(out_shape),
        grid=(b, t // tm),
        in_specs=in_specs,
        out_specs=tuple(out_specs),
        compiler_params=_cparams(2),
        name="proj_" + (out_kind or "first") + ("_in" if inp is not None else "_last"),
    )(*args)
    if out is None:
        return None, res[0]
    if inp is None:
        return res[0], None
    return res


def _group_scan(a, b, reverse):
    rows, lanes = a.shape
    ng = rows // 8
    a = a.reshape(ng, 8, lanes)
    b = b.reshape(ng, 8, lanes)
    sub = lax.broadcasted_iota(jnp.int32, (1, 8, lanes), 1)
    for dd in (1, 2, 4):
        if reverse:
            m = sub <= 7 - dd
            sh = 8 - dd
        else:
            m = sub >= dd
            sh = dd
        a_sh = pltpu.roll(a, sh, axis=1)
        b_sh = pltpu.roll(b, sh, axis=1)
        b = b + a * jnp.where(m, b_sh, 0.0)
        a = a * jnp.where(m, a_sh, 1.0)
    return a.reshape(rows, lanes), b.reshape(rows, lanes)


def _carry_chain(a, b, carry, reverse):
    ng = a.shape[0] // 8
    outs = [None] * ng
    order = range(ng - 1, -1, -1) if reverse else range(ng)
    for gi in order:
        hg = b[8 * gi:8 * gi + 8] + a[8 * gi:8 * gi + 8] * carry
        carry = hg[0:1] if reverse else hg[7:8]
        outs[gi] = hg
    return jnp.concatenate(outs, axis=0), carry


def _scan_chunk(a, b, carry, reverse):
    a, b = _group_scan(a, b, reverse)
    return _carry_chain(a, b, carry, reverse)


def _sqrt_nonneg(v):
    return jnp.where(v > 0.0, v * lax.rsqrt(v), 0.0)


def _lru_kernel(xa_ref, ga_ref, xca_ref, gca_ref, cw_ref, cb_ref, wg_ref, bg_ref, lam_ref,
                o_ref, oc_ref, zs_ref, hb_ref):
    cw = cw_ref[...]
    cb = cb_ref[...]
    nsp2 = []
    for d in (0, 1):
        y = -lam_ref[d]
        nsp2.append((-LRU_C * LOG2E) * (jnp.maximum(y, 0.0) + jnp.log1p(jnp.exp(-jnp.abs(y)))))

    def run_seq(x_ref, g_ref, out_ref, carry_f, carry_r):
        tn = x_ref.shape[0]
        rn = min(LRU_CHUNK, tn)
        nch = tn // rn
        assert nch == 1 or nch % 2 == 0

        def conv_body(c, carry):
            r0 = pl.multiple_of(c * rn, rn)
            p0 = pl.multiple_of(jnp.maximum(r0 - 16, 0), 16)
            n0 = pl.multiple_of(jnp.minimum(r0 + rn, tn - 16), 16)
            cur = x_ref[pl.ds(r0, rn), :].astype(F32)
            prev = jnp.where(c > 0, x_ref[pl.ds(p0, 16), :].astype(F32), 0.0)
            nxt = jnp.where(c < nch - 1, x_ref[pl.ds(n0, 16), :].astype(F32), 0.0)
            ext = jnp.concatenate([prev, cur, nxt], axis=0)
            z = cb + cw[0:1] * pltpu.roll(ext, 2, axis=0)[16:16 + rn]
            z = z + cw[1:2] * pltpu.roll(ext, 1, axis=0)[16:16 + rn]
            z = z + cw[2:3] * cur
            z = z + cw[3:4] * pltpu.roll(ext, rn + 31, axis=0)[16:16 + rn]
            zs_ref[pl.ds(r0, rn), :] = z
            return carry

        lax.fori_loop(0, nch, conv_body, 0)

        def coeffs(r0, d):
            z = zs_ref[pl.ds(r0, rn), :]
            e = jnp.exp2(jnp.dot(z.astype(BF16), wg_ref[d], preferred_element_type=F32) + bg_ref[d])
            r = 1.0 / (1.0 + e[:, :LANES])
            i = 1.0 / (1.0 + e[:, LANES:])
            a = jnp.exp2(nsp2[d] * r)
            return a, _sqrt_nonneg(1.0 - a * a) * (i * z)

        def gate(r0):
            return _silu(g_ref[pl.ds(r0, rn), :].astype(F32))

        if nch == 1:
            a, b = coeffs(0, 0)
            hf, carry_f = _scan_chunk(a, b, carry_f, False)
            a, b = coeffs(0, 1)
            hr, carry_r = _scan_chunk(a, b, carry_r, True)
            out_ref[...] = ((hf + hr) * gate(0)).astype(out_ref.dtype)
            return carry_f, carry_r

        def step(c, carry, combine):
            cf, cr = carry
            r0 = pl.multiple_of(c * rn, rn)
            r1 = pl.multiple_of((nch - 1 - c) * rn, rn)
            a, b = coeffs(r0, 0)
            hf, cf = _scan_chunk(a, b, cf, False)
            a, b = coeffs(r1, 1)
            hr, cr = _scan_chunk(a, b, cr, True)
            if combine:
                out_ref[pl.ds(r0, rn), :] = ((hf + hb_ref[pl.ds(r0, rn), :]) * gate(r0)).astype(out_ref.dtype)
                out_ref[pl.ds(r1, rn), :] = ((hb_ref[pl.ds(r1, rn), :] + hr) * gate(r1)).astype(out_ref.dtype)
            else:
                hb_ref[pl.ds(r0, rn), :] = hf
                hb_ref[pl.ds(r1, rn), :] = hr
            return cf, cr

        carry = lax.fori_loop(0, nch // 2, functools.partial(step, combine=False), (carry_f, carry_r),
                              unroll=LRU_UNROLL)
        return lax.fori_loop(nch // 2, nch, functools.partial(step, combine=True), carry,
                             unroll=LRU_UNROLL)

    zero = jnp.zeros((1, LANES), F32)
    cf, cr = run_seq(xca_ref, gca_ref, oc_ref, zero, zero)
    run_seq(xa_ref, ga_ref, o_ref, cf, cr)


def _lru_branch(u, uc, conv_w, conv_b, wg, bg, lam):
    b, t, _ = u.shape
    tc = uc.shape[1]
    w = conv_w.shape[1]
    nh = w // LANES
    ga0 = 2 * nh
    return pl.pallas_call(
        _lru_kernel,
        out_shape=(jax.ShapeDtypeStruct((b, t, w), BF16), jax.ShapeDtypeStruct((b, tc, w), BF16)),
        grid=(b, nh),
        in_specs=[
            pl.BlockSpec((None, t, LANES), lambda bi, h: (bi, 0, h)),
            pl.BlockSpec((None, t, LANES), lambda bi, h: (bi, 0, ga0 + h)),
            pl.BlockSpec((None, tc, LANES), lambda bi, h: (bi, 0, h)),
            pl.BlockSpec((None, tc, LANES), lambda bi, h: (bi, 0, ga0 + h)),
            pl.BlockSpec((conv_w.shape[0], LANES), lambda bi, h: (0, h)),
            pl.BlockSpec((1, LANES), lambda bi, h: (0, h)),
            pl.BlockSpec((2, None, LANES, 2 * LANES), lambda bi, h: (0, h, 0, 0)),
            pl.BlockSpec((2, None, 1, 2 * LANES), lambda bi, h: (0, h, 0, 0)),
            pl.BlockSpec((2, None, 1, LANES), lambda bi, h: (0, h, 0, 0)),
        ],
        out_specs=(pl.BlockSpec((None, t, LANES), lambda bi, h: (bi, 0, h)),
                   pl.BlockSpec((None, tc, LANES), lambda bi, h: (bi, 0, h))),
        scratch_shapes=[pltpu.VMEM((t, LANES), F32), pltpu.VMEM((t, LANES), F32)],
        compiler_params=_cparams(2),
        name="rglru",
    )(u, u, uc, uc, conv_w, conv_b, wg, bg, lam)


def _w12_kernel(cs_ref, wf_ref, o_ref):
    gd = wf_ref.shape[0]
    for half in range(gd // LANES):
        wf = wf_ref[:, half * LANES:(half + 1) * LANES]
        o_ref[half, :, :LANES] = jnp.dot(cs_ref[:gd, :], wf, preferred_element_type=F32,
                                         precision=lax.Precision.HIGHEST).astype(o_ref.dtype)
        o_ref[half, :, LANES:] = jnp.dot(cs_ref[gd:, :], wf, preferred_element_type=F32,
                                         precision=lax.Precision.HIGHEST).astype(o_ref.dtype)


def _fourier_weights(fnet_w):
    ne, ng, gd, _ = fnet_w.shape
    nhalf = gd // LANES
    idx = np.arange(gd)
    ang = 2.0 * np.pi * ((idx[:, None] * idx[None, :]) % gd) / gd
    cs = np.concatenate([np.cos(ang), np.sin(ang)], axis=0) / math.sqrt(gd)
    out = pl.pallas_call(
        _w12_kernel,
        out_shape=jax.ShapeDtypeStruct((ne, ng, nhalf, gd, 2 * LANES), BF16),
        grid=(ne, ng),
        in_specs=[pl.BlockSpec((2 * gd, gd), lambda e, g: (0, 0)),
                  pl.BlockSpec((None, None, gd, gd), lambda e, g: (e, g, 0, 0))],
        out_specs=pl.BlockSpec((None, None, nhalf, gd, 2 * LANES), lambda e, g: (e, g, 0, 0, 0)),
        compiler_params=_cparams(2),
        name="fourier_weights",
    )(jnp.asarray(cs, F32), fnet_w)
    return out.reshape(ne, ng * nhalf, gd, 2 * LANES)


def _chanmix_kernel(xb_ref, w_ref, ur_ref, ui_ref):
    ncb, gd, _ = w_ref.shape
    per_group = gd // LANES
    for cb in range(ncb):
        g = cb // per_group
        r = jnp.dot(xb_ref[:, g * gd:(g + 1) * gd], w_ref[cb], preferred_element_type=F32)
        ur_ref[:, cb * LANES:(cb + 1) * LANES] = r[:, :LANES]
        ui_ref[:, cb * LANES:(cb + 1) * LANES] = r[:, LANES:]


def _chanmix(u, w12):
    b, t, _ = u.shape
    ncb, gd, _ = w12.shape
    w = ncb * LANES
    tm = min(ROW_TILE, t)
    out = jax.ShapeDtypeStruct((b, t, w), F32)
    return pl.pallas_call(
        _chanmix_kernel,
        out_shape=(out, out),
        grid=(b, t // tm),
        in_specs=[pl.BlockSpec((None, tm, w), lambda bi, i: (bi, i, 1)),
                  pl.BlockSpec((ncb, gd, 2 * LANES), lambda bi, i: (0, 0, 0))],
        out_specs=(pl.BlockSpec((None, tm, w), lambda bi, i: (bi, i, 0)),
                   pl.BlockSpec((None, tm, w), lambda bi, i: (bi, i, 0))),
        compiler_params=_cparams(2),
        name="fourier_chanmix",
    )(u, w12)


def _dft_a_kernel(xb_ref, w_ref, ea_ref, twr_ref, twi_ref, br_ref, bi_ref, ur_s, ui_s):
    n2, nj, gd = xb_ref.shape
    u = jnp.dot(xb_ref[...].reshape(n2 * nj, gd), w_ref[...], preferred_element_type=F32)
    ur_s[...] = u[:, :LANES]
    ui_s[...] = u[:, LANES:]
    xr = jnp.concatenate([ur_s[pl.ds(j, n2, stride=nj), :] for j in range(nj)], axis=1)
    xi = jnp.concatenate([ui_s[pl.ds(j, n2, stride=nj), :] for j in range(nj)], axis=1)
    rhs = jnp.concatenate([xr, xi], axis=0).astype(BF16)
    res = jnp.dot(ea_ref[...], rhs, preferred_element_type=F32)
    for j in range(nj):
        pr = res[:n2, j * LANES:(j + 1) * LANES]
        pi = res[n2:, j * LANES:(j + 1) * LANES]
        tr = twr_ref[j]
        ti = twi_ref[j]
        br_ref[j] = (pr * tr - pi * ti).astype(br_ref.dtype)
        bi_ref[j] = (pr * ti + pi * tr).astype(bi_ref.dtype)


def _dft_b_kernel(br_ref, bi_ref, gb_ref, eb_ref, o_ref, sr_ref, si_ref):
    n1, nj, _ = br_ref.shape
    sr_ref[...] = br_ref[...].astype(F32).reshape(n1 * nj, LANES)
    si_ref[...] = bi_ref[...].astype(F32).reshape(n1 * nj, LANES)
    xr = jnp.concatenate([sr_ref[pl.ds(j, n1, stride=nj), :] for j in range(nj)], axis=1)
    xi = jnp.concatenate([si_ref[pl.ds(j, n1, stride=nj), :] for j in range(nj)], axis=1)
    rhs = jnp.concatenate([xr, xi], axis=0).astype(BF16)
    res = jnp.dot(eb_ref[...], rhs, preferred_element_type=F32)
    for j in range(nj):
        sr_ref[pl.ds(j, n1, stride=nj), :] = res[:, j * LANES:(j + 1) * LANES]
    fy = sr_ref[...].reshape(n1, nj, LANES)
    o_ref[...] = (fy * _silu(gb_ref[...].astype(F32))).astype(o_ref.dtype)


def _dft_consts(t):
    n2 = FFT_N2
    n1 = t // n2
    i2 = np.arange(n2)
    a2 = 2.0 * np.pi * ((i2[:, None] * i2[None, :]) % n2) / n2
    er, ei = np.cos(a2), np.sin(a2)
    ea = np.block([[er, -ei], [ei, er]])
    i1 = np.arange(n1)
    atw = 2.0 * np.pi * ((i1[:, None] * i2[None, :]) % t) / t
    a1 = 2.0 * np.pi * ((i1[:, None] * i1[None, :]) % n1) / n1
    eb = np.concatenate([np.cos(a1), -np.sin(a1)], axis=1) / math.sqrt(t)
    twr = jnp.broadcast_to(jnp.asarray(np.cos(atw), F32)[:, :, None], (n1, n2, LANES))
    twi = jnp.broadcast_to(jnp.asarray(np.sin(atw), F32)[:, :, None], (n1, n2, LANES))
    return jnp.asarray(ea, F32).astype(BF16), twr, twi, jnp.asarray(eb, F32).astype(BF16)


def _seq_fourier(u, w12, consts):
    b, t, n_in = u.shape
    ncb, gd, _ = w12.shape
    w = ncb * LANES
    n2 = FFT_N2
    n1 = t // n2
    nj = FFT_J
    ea, twr, twi, eb = consts
    u4 = u.reshape(b, n2, n1, n_in)
    xb0 = w // gd
    per_group = gd // LANES
    mid = jax.ShapeDtypeStruct((b, n1, n2, w), BF16)
    br, bi = pl.pallas_call(
        _dft_a_kernel,
        out_shape=(mid, mid),
        grid=(n1 // nj, b, ncb),
        in_specs=[
            pl.BlockSpec((None, n2, nj, gd), lambda tb, bi_, cb: (bi_, 0, tb, xb0 + cb // per_group)),
            pl.BlockSpec((None, gd, 2 * LANES), lambda tb, bi_, cb: (cb, 0, 0)),
            pl.BlockSpec((2 * n2, 2 * n2), lambda tb, bi_, cb: (0, 0)),
            pl.BlockSpec((nj, n2, LANES), lambda tb, bi_, cb: (tb, 0, 0)),
            pl.BlockSpec((nj, n2, LANES), lambda tb, bi_, cb: (tb, 0, 0)),
        ],
        out_specs=(pl.BlockSpec((None, nj, n2, LANES), lambda tb, bi_, cb: (bi_, tb, 0, cb)),
                   pl.BlockSpec((None, nj, n2, LANES), lambda tb, bi_, cb: (bi_, tb, 0, cb))),
        scratch_shapes=[pltpu.VMEM((n2 * nj, LANES), F32), pltpu.VMEM((n2 * nj, LANES), F32)],
        compiler_params=_cparams(3),
        name="dft_stage_a",
    )(u4, w12, ea, twr, twi)
    gb0 = (n_in - w) // LANES
    out = pl.pallas_call(
        _dft_b_kernel,
        out_shape=jax.ShapeDtypeStruct((b, n1, n2, w), BF16),
        grid=(b, n2 // nj, w // LANES),
        in_specs=[
            pl.BlockSpec((None, n1, nj, LANES), lambda bi_, kb, cb: (bi_, 0, kb, cb)),
            pl.BlockSpec((None, n1, nj, LANES), lambda bi_, kb, cb: (bi_, 0, kb, cb)),
            pl.BlockSpec((None, n1, nj, LANES), lambda bi_, kb, cb: (bi_, 0, kb, gb0 + cb)),
            pl.BlockSpec((n1, 2 * n1), lambda bi_, kb, cb: (0, 0)),
        ],
        out_specs=pl.BlockSpec((None, n1, nj, LANES), lambda bi_, kb, cb: (bi_, 0, kb, cb)),
        scratch_shapes=[pltpu.VMEM((n1 * nj, LANES), F32), pltpu.VMEM((n1 * nj, LANES), F32)],
        compiler_params=_cparams(3),
        name="dft_stage_b",
    )(br, bi, u.reshape(b, n1, n2, n_in), eb)
    return out.reshape(b, t, w)


def _dft_dense_kernel(ur_ref, ui_ref, gb_ref, ec_ref, o_ref):
    rhs = jnp.concatenate([ur_ref[...], ui_ref[...]], axis=0).astype(BF16)
    fy = jnp.dot(ec_ref[...], rhs, preferred_element_type=F32)
    o_ref[...] = (fy * _silu(gb_ref[...].astype(F32))).astype(o_ref.dtype)


def _seq_dft_dense(ur, ui, u):
    b, t, w = ur.shape
    gb_blk = u.shape[-1] // w - 1
    it = np.arange(t)
    ang = 2.0 * np.pi * ((it[:, None] * it[None, :]) % t) / t
    ec = jnp.asarray(np.concatenate([np.cos(ang), -np.sin(ang)], axis=1) / math.sqrt(t), F32).astype(BF16)
    return pl.pallas_call(
        _dft_dense_kernel,
        out_shape=jax.ShapeDtypeStruct((b, t, w), BF16),
        grid=(b,),
        in_specs=[pl.BlockSpec((None, t, w), lambda bi: (bi, 0, 0)),
                  pl.BlockSpec((None, t, w), lambda bi: (bi, 0, 0)),
                  pl.BlockSpec((None, t, w), lambda bi: (bi, 0, gb_blk)),
                  pl.BlockSpec((t, 2 * t), lambda bi: (0, 0))],
        out_specs=pl.BlockSpec((None, t, w), lambda bi: (bi, 0, 0)),
        compiler_params=_cparams(1),
        name="dft_dense",
    )(ur, ui, u, ec)


def _ret_kernel(lg_ref, q_ref, k_ref, v_ref, *rest, reverse, final):
    if final:
        op_ref, s0_ref, o_ref, se_ref, s_ref, dm_ref, qd_ref, kd_ref = rest
    else:
        s0_ref, o_ref, se_ref, s_ref, dm_ref, qd_ref, kd_ref = rest
    nh, dk, dv = s_ref.shape
    cr = q_ref.shape[0]
    d = 1 if reverse else 0

    @pl.when(pl.program_id(1) == 0)
    def _init():
        s_ref[...] = s0_ref[...]
        ii = lax.broadcasted_iota(jnp.int32, (cr, cr), 0).astype(F32)
        jj = lax.broadcasted_iota(jnp.int32, (cr, cr), 1).astype(F32)
        diff = jj - ii
        rq = lax.broadcasted_iota(jnp.int32, (cr, dv), 0).astype(F32)
        rk = lax.broadcasted_iota(jnp.int32, (cr, dk), 0).astype(F32)
        for h in range(nh):
            lgh = lg_ref[d, h]
            if reverse:
                mask = diff > 0
                expo = diff
                qe = cr - rq
                ke = rk
            else:
                mask = diff <= 0
                expo = -diff
                qe = rq + 1.0
                ke = (cr - 1.0) - rk
            dm_ref[h] = jnp.where(mask, jnp.exp(lgh * jnp.where(mask, expo, 0.0)), 0.0)
            qd_ref[h] = jnp.exp(lgh * qe)
            kd_ref[h] = jnp.exp(lgh * ke)

    for h in range(nh):
        cdec = jnp.exp(jnp.full((1, dv), lg_ref[d, h] * cr, F32))
        qh = q_ref[:, h * dk:(h + 1) * dk]
        kh = k_ref[:, h * dk:(h + 1) * dk]
        vh = v_ref[:, h * dv:(h + 1) * dv]
        s = lax.dot_general(qh, kh, (((1,), (1,)), ((), ())), preferred_element_type=F32)
        o = jnp.dot((s * dm_ref[h]).astype(BF16), vh, preferred_element_type=F32)
        st = s_ref[h]
        o = o + qd_ref[h] * jnp.dot(qh, st.astype(BF16), preferred_element_type=F32)
        kdec = (kh.astype(F32) * kd_ref[h]).astype(BF16)
        s_ref[h] = cdec * st + lax.dot_general(kdec, vh, (((0,), (0,)), ((), ())),
                                               preferred_element_type=F32)
        if final:
            o = o + op_ref[:, h * dv:(h + 1) * dv].astype(F32)
            mu = jnp.mean(o, axis=-1, keepdims=True)
            oc = o - mu
            var = jnp.mean(oc * oc, axis=-1, keepdims=True)
            o = oc * lax.rsqrt(var + EPS)
        o_ref[:, h * dv:(h + 1) * dv] = o.astype(o_ref.dtype)

    @pl.when(pl.program_id(1) == pl.num_programs(1) - 1)
    def _fin():
        se_ref[...] = s_ref[...]


def _ret_pass(u, lg, s0, o_prev, reverse, final):
    b, t, _ = u.shape
    _, nh, dk, dv = s0.shape
    cr = min(RET_CHUNK, t)
    nc = t // cr
    qk = nh * dk
    vw = nh * dv
    if reverse:
        ci = lambda i: nc - 1 - i
    else:
        ci = lambda i: i
    in_specs = [
        pl.BlockSpec(memory_space=pltpu.SMEM),
        pl.BlockSpec((None, cr, qk), lambda bi, i: (bi, ci(i), 0)),
        pl.BlockSpec((None, cr, qk), lambda bi, i: (bi, ci(i), 1)),
        pl.BlockSpec((None, cr, vw), lambda bi, i: (bi, ci(i), (2 * qk) // vw)),
    ]
    args = [lg, u, u, u]
    if final:
        in_specs.append(pl.BlockSpec((None, cr, vw), lambda bi, i: (bi, ci(i), 0)))
        args.append(o_prev)
    in_specs.append(pl.BlockSpec((None, nh, dk, dv), lambda bi, i: (bi, 0, 0, 0)))
    args.append(s0)
    return pl.pallas_call(
        functools.partial(_ret_kernel, reverse=reverse, final=final),
        out_shape=(jax.ShapeDtypeStruct((b, t, vw), BF16),
                   jax.ShapeDtypeStruct((b, nh, dk, dv), F32)),
        grid=(b, nc),
        in_specs=in_specs,
        out_specs=(pl.BlockSpec((None, cr, vw), lambda bi, i: (bi, ci(i), 0)),
                   pl.BlockSpec((None, nh, dk, dv), lambda bi, i: (bi, 0, 0, 0))),
        scratch_shapes=[
            pltpu.VMEM((nh, dk, dv), F32),
            pltpu.VMEM((nh, cr, cr), F32),
            pltpu.VMEM((nh, cr, dv), F32),
            pltpu.VMEM((nh, cr, dk), F32),
        ],
        compiler_params=_cparams(2),
        name="retention_rev" if reverse else "retention_fwd",
    )(*args)


def _retention(u, uc, lg):
    b = u.shape[0]
    nh = RET_HEADS
    dk = u.shape[-1] // (6 * nh)
    zero = jnp.zeros((b, nh, dk, 2 * dk), F32)
    oc_r, s_r = _ret_pass(uc, lg, zero, None, True, False)
    mc, s_f = _ret_pass(uc, lg, zero, oc_r, False, True)
    o_r, _ = _ret_pass(u, lg, s_r, None, True, False)
    m, _ = _ret_pass(u, lg, s_f, o_r, False, True)
    return m, mc


def _rope_tables(t):
    nf = LANES // 2
    inv = ROPE_THETA ** (-jnp.arange(nf, dtype=F32) / nf)
    ar = jnp.arange(t // GRID_W, dtype=F32)[:, None] * inv
    ac = jnp.arange(GRID_W, dtype=F32)[:, None] * inv
    dup = lambda v: jnp.concatenate([v, v], axis=1)
    sgn = lambda v: jnp.concatenate([-v, v], axis=1)
    return dup(jnp.cos(ar)), sgn(jnp.sin(ar)), dup(jnp.cos(ac)), sgn(jnp.sin(ac))


def kernel(x, c, ctx, c_ctx, mod_w, mod_b, pre_g, post_g, mix_w_in, mix_w_out, conv_w, conv_b,
           lru_wa, lru_ba, lru_wx, lru_bx, lru_lam, fnet_w, ret_w_in, ret_w_out, ret_log_gamma):
    b, t, d = x.shape
    tc = ctx.shape[1]
    depth = mod_w.shape[0]
    assert t % ROW_TILE == 0 and t % (FFT_N2 * FFT_J) == 0 and (t // FFT_N2) % FFT_J == 0
    assert tc % GRID_W == 0 and tc % 16 == 0 and tc <= ROW_TILE and b + 1 <= 8

    cond8 = jnp.zeros((8, d), F32).at[:b].set(c).at[b].set(c_ctx)
    mods = _ada_mod(cond8, mod_w, mod_b)

    w12 = _fourier_weights(fnet_w)
    dft_consts = _dft_consts(t)
    qk = ret_w_in.shape[-1] // 6
    rope_l = _rope_tables(t) + (2 * qk,)
    ones = lambda n: jnp.ones((n, LANES), F32)
    zeros = lambda n: jnp.zeros((n, LANES), F32)
    rope_c = (ones(tc // GRID_W), zeros(tc // GRID_W), ones(GRID_W), zeros(GRID_W), 2 * qk)

    nh_l = LRU_HEADS
    hd = lru_wa.shape[-1]
    wg_all = (jnp.concatenate([lru_wa, lru_wx], axis=-1) * (-LOG2E)).astype(BF16)
    bg_all = jnp.concatenate([lru_ba.reshape(-1, 2, nh_l, 1, hd),
                              lru_bx.reshape(-1, 2, nh_l, 1, hd)], axis=-1) * (-LOG2E)
    lam_all = lru_lam.reshape(-1, 2, nh_l, 1, hd)

    kscale = jnp.concatenate([jnp.ones((qk,), F32), jnp.full((qk,), (qk // RET_HEADS) ** -0.5, F32),
                              jnp.ones((4 * qk,), F32)])

    def w_in_of(layer):
        if layer % 2 == 0:
            return mix_w_in[layer // 2].astype(BF16)
        return (ret_w_in[layer // 2] * kscale).astype(BF16)

    def mod_of(layer, latent):
        m_l = mods[layer]
        parts = (m_l[:, :d], m_l[:, d:2 * d], m_l[:, 2 * d:])
        if latent:
            return tuple(v[:b].reshape(b, 1, d) for v in parts)
        return tuple(jnp.broadcast_to(v[b].reshape(1, 1, d), (b, 1, d)) for v in parts)

    def inp_of(layer, latent):
        shift, scale, _ = mod_of(layer, latent)
        rope = None if layer % 2 == 0 else (rope_l if latent else rope_c)
        return (pre_g[layer].reshape(1, d), scale, shift, w_in_of(layer), rope)

    xc = ctx
    _, u = _proj(x, inp=inp_of(0, True))
    _, uc = _proj(xc, inp=inp_of(0, False))
    for layer in range(depth):
        last = layer == depth - 1
        post = post_g[layer].reshape(1, d)
        gt_l = mod_of(layer, True)[2]
        gt_c = mod_of(layer, False)[2]
        if layer % 2 == 0:
            e = layer // 2
            w_out = mix_w_out[e].astype(BF16)
            ma, mca = _lru_branch(u, uc, conv_w[e], conv_b[e].reshape(1, -1),
                                  wg_all[e], bg_all[e], lam_all[e])
            mb = _seq_fourier(u, w12[e], dft_consts)
            out_l = ("even", ma, mb, w_out, post, gt_l)
            if not last:
                urc, uic = _chanmix(uc, w12[e])
                out_c = ("even", mca, _seq_dft_dense(urc, uic, uc), w_out, post, gt_c)
        else:
            j = layer // 2
            w_out = ret_w_out[j].astype(BF16)
            m, mc = _retention(u, uc, ret_log_gamma[j])
            out_l = ("odd", m, u, w_out, post, gt_l)
            out_c = ("odd", mc, uc, w_out, post, gt_c)
        if last:
            x, _ = _proj(x, out=out_l)
        else:
            x, u = _proj(x, out=out_l, inp=inp_of(layer + 1, True))
            xc, uc = _proj(xc, out=out_c, inp=inp_of(layer + 1, False))
    return x
```

```python
import functools
import math

import numpy as np
import jax
import jax.numpy as jnp
from jax import lax
from jax.experimental import pallas as pl
from jax.experimental.pallas import tpu as pltpu

F32 = jnp.float32
BF16 = jnp.bfloat16

EPS = 1e-6
LOG2E = math.log2(math.e)
LRU_C = 8.0
LRU_HEADS = 8
FNET_GROUPS = 4
RET_HEADS = 4
GRID_W = 64
ROPE_THETA = 10000.0

LANES = 128
VMEM_LIMIT = 56 * 1024 * 1024

ROW_TILE = 512
LRU_CHUNK = 256
LRU_UNROLL = 4
RET_CHUNK = 256
FFT_N2 = 128
FFT_J = 16


def _cparams(n_axes):
    return pltpu.CompilerParams(dimension_semantics=("arbitrary",) * n_axes,
                                vmem_limit_bytes=VMEM_LIMIT)


def _silu(v):
    return v * jax.nn.sigmoid(v)


def _ada_kernel(cond_ref, w_ref, b_ref, o_ref):
    s = _silu(cond_ref[...])
    o_ref[...] = jnp.dot(s, w_ref[...], preferred_element_type=F32,
                         precision=lax.Precision.HIGHEST) + b_ref[...]


def _ada_mod(cond8, mod_w, mod_b):
    depth, d, n3 = mod_w.shape
    tn = d
    return pl.pallas_call(
        _ada_kernel,
        out_shape=jax.ShapeDtypeStruct((depth, 8, n3), F32),
        grid=(depth, n3 // tn),
        in_specs=[
            pl.BlockSpec((8, d), lambda l, j: (0, 0)),
            pl.BlockSpec((None, d, tn), lambda l, j: (l, 0, j)),
            pl.BlockSpec((None, 1, tn), lambda l, j: (l, 0, j)),
        ],
        out_specs=pl.BlockSpec((None, 8, tn), lambda l, j: (l, 0, j)),
        compiler_params=_cparams(2),
        name="ada_mod",
    )(cond8, mod_w, mod_b.reshape(depth, 1, n3))


def _proj_kernel(*refs, out_kind, has_in, rope_cols, chunk):
    it = iter(refs)
    if out_kind:
        a_ref, b_ref = next(it), next(it)
    x_ref = next(it)
    if out_kind:
        wo_ref, pg_ref, gate_ref = next(it), next(it), next(it)
    if has_in:
        g_ref, sc_ref, sh_ref, wi_ref = next(it), next(it), next(it), next(it)
        if rope_cols:
            crow_ref, srow_ref, ccol_ref, scol_ref = next(it), next(it), next(it), next(it)
    if out_kind:
        xo_ref = next(it)
    if has_in:
        u_ref = next(it)

    x = x_ref[...]
    if out_kind == "even":
        wa = a_ref.shape[-1]
        y = jnp.dot(a_ref[...], wo_ref[:wa, :], preferred_element_type=F32)
        y = y + jnp.dot(b_ref[...], wo_ref[wa:, :], preferred_element_type=F32)
    elif out_kind == "odd":
        m = (a_ref[...].astype(F32) * _silu(b_ref[...].astype(F32))).astype(BF16)
        y = jnp.dot(m, wo_ref[...], preferred_element_type=F32)
    if out_kind:
        ms = jnp.mean(y * y, axis=-1, keepdims=True)
        x = x + gate_ref[...] * ((y * lax.rsqrt(ms + EPS)) * pg_ref[...])
        xo_ref[...] = x
    if not has_in:
        return

    ms = jnp.mean(x * x, axis=-1, keepdims=True)
    h = (x * lax.rsqrt(ms + EPS)) * g_ref[...]
    h = h * (1.0 + sc_ref[...]) + sh_ref[...]
    hb = h.astype(BF16)
    if rope_cols:
        nrow = crow_ref.shape[0]
        expand = lambda r: jnp.concatenate(
            [jnp.broadcast_to(r[g:g + 1, :], (GRID_W, LANES)) for g in range(nrow)], axis=0)
        repeat = lambda r: jnp.concatenate([r[...]] * nrow, axis=0)
        cos_t = (expand(crow_ref), repeat(ccol_ref))
        sin_t = (expand(srow_ref), repeat(scol_ref))
    n_out = u_ref.shape[-1]
    for c0 in range(0, n_out, chunk):
        r = jnp.dot(hb, wi_ref[:, c0:c0 + chunk], preferred_element_type=F32)
        if c0 < rope_cols:
            parts = []
            for s0 in range(0, chunk, LANES):
                axis = ((c0 + s0) // LANES) % 2
                ra = r[:, s0:s0 + LANES]
                parts.append(ra * cos_t[axis] + pltpu.roll(ra, LANES // 2, axis=1) * sin_t[axis])
            r = jnp.concatenate(parts, axis=1)
        u_ref[:, c0:c0 + chunk] = r.astype(u_ref.dtype)


def _proj(x, out=None, inp=None):
    b, t, d = x.shape
    tm = min(ROW_TILE, t)
    row = lambda bi, i: (bi, i, 0)
    const2 = lambda bi, i: (0, 0)
    per_b = lambda bi, i: (bi, 0, 0)
    in_specs, args, out_shape, out_specs = [], [], [], []
    out_kind = None
    if out is not None:
        out_kind, a, bb, w_out, pg, gate = out
        ka = a.shape[-1]
        in_specs.append(pl.BlockSpec((None, tm, ka), row))
        if out_kind == "even":
            in_specs.append(pl.BlockSpec((None, tm, bb.shape[-1]), row))
        else:
            g_blk = bb.shape[-1] // ka - 1
            in_specs.append(pl.BlockSpec((None, tm, ka), lambda bi, i: (bi, i, g_blk)))
        args += [a, bb]
    in_specs.append(pl.BlockSpec((None, tm, d), row))
    args.append(x)
    if out is not None:
        in_specs += [pl.BlockSpec(w_out.shape, const2, pipeline_mode=pl.Buffered(1)),
                     pl.BlockSpec((1, d), const2), pl.BlockSpec((None, 1, d), per_b)]
        args += [w_out, pg, gate]
        out_shape.append(jax.ShapeDtypeStruct((b, t, d), F32))
        out_specs.append(pl.BlockSpec((None, tm, d), row))
    rope_cols = 0
    if inp is not None:
        g, scale, shift, w_in, rope = inp
        n = w_in.shape[1]
        in_specs += [pl.BlockSpec((1, d), const2), pl.BlockSpec((None, 1, d), per_b),
                     pl.BlockSpec((None, 1, d), per_b),
                     pl.BlockSpec((d, n), const2, pipeline_mode=pl.Buffered(1))]
        args += [g, scale, shift, w_in]
        if rope is not None:
            crow, srow, ccol, scol, rope_cols = rope
            assert tm % GRID_W == 0
            nrow = tm // GRID_W
            in_specs += [pl.BlockSpec((nrow, LANES), lambda bi, i: (i, 0)),
                         pl.BlockSpec((nrow, LANES), lambda bi, i: (i, 0)),
                         pl.BlockSpec((GRID_W, LANES), const2), pl.BlockSpec((GRID_W, LANES), const2)]
            args += [crow, srow, ccol, scol]
        out_shape.append(jax.ShapeDtypeStruct((b, t, n), BF16))
        out_specs.append(pl.BlockSpec((None, tm, n), row))
    res = pl.pallas_call(
        functools.partial(_proj_kernel, out_kind=out_kind, has_in=inp is not None,
                          rope_cols=rope_cols, chunk=2 * LANES),
        out_shape=tuple(out_shape),
        grid=(b, t // tm),
        in_specs=in_specs,
        out_specs=tuple(out_specs),
        compiler_params=_cparams(2),
        name="proj_" + (out_kind or "first") + ("_in" if inp is not None else "_last"),
    )(*args)
    if out is None:
        return None, res[0]
    if inp is None:
        return res[0], None
    return res


def _group_scan(a, b, reverse):
    rows, lanes = a.shape
    ng = rows // 8
    a = a.reshape(ng, 8, lanes)
    b = b.reshape(ng, 8, lanes)
    sub = lax.broadcasted_iota(jnp.int32, (1, 8, lanes), 1)
    for dd in (1, 2, 4):
        if reverse:
            m = sub <= 7 - dd
            sh = 8 - dd
        else:
            m = sub >= dd
            sh = dd
        a_sh = pltpu.roll(a, sh, axis=1)
        b_sh = pltpu.roll(b, sh, axis=1)
        b = b + a * jnp.where(m, b_sh, 0.0)
        a = a * jnp.where(m, a_sh, 1.0)
    return a.reshape(rows, lanes), b.reshape(rows, lanes)


def _carry_chain(a, b, carry, reverse):
    ng = a.shape[0] // 8
    outs = [None] * ng
    order = range(ng - 1, -1, -1) if reverse else range(ng)
    for gi in order:
        hg = b[8 * gi:8 * gi + 8] + a[8 * gi:8 * gi + 8] * carry
        carry = hg[0:1] if reverse else hg[7:8]
        outs[gi] = hg
    return jnp.concatenate(outs, axis=0), carry


def _scan_chunk(a, b, carry, reverse):
    a, b = _group_scan(a, b, reverse)
    return _carry_chain(a, b, carry, reverse)


def _sqrt_nonneg(v):
    return jnp.where(v > 0.0, v * lax.rsqrt(v), 0.0)


def _lru_kernel(xa_ref, ga_ref, xca_ref, gca_ref, cw_ref, cb_ref, wg_ref, bg_ref, lam_ref,
                o_ref, oc_ref, zs_ref, hb_ref):
    cw = cw_ref[...]
    cb = cb_ref[...]
    nsp2 = []
    for d in (0, 1):
        y = -lam_ref[d]
        nsp2.append((-LRU_C * LOG2E) * (jnp.maximum(y, 0.0) + jnp.log1p(jnp.exp(-jnp.abs(y)))))

    def run_seq(x_ref, g_ref, out_ref, carry_f, carry_r):
        tn = x_ref.shape[0]
        rn = min(LRU_CHUNK, tn)
        nch = tn // rn
        assert nch == 1 or nch % 2 == 0

        def conv_body(c, carry):
            r0 = pl.multiple_of(c * rn, rn)
            p0 = pl.multiple_of(jnp.maximum(r0 - 16, 0), 16)
            n0 = pl.multiple_of(jnp.minimum(r0 + rn, tn - 16), 16)
            cur = x_ref[pl.ds(r0, rn), :].astype(F32)
            prev = jnp.where(c > 0, x_ref[pl.ds(p0, 16), :].astype(F32), 0.0)
            nxt = jnp.where(c < nch - 1, x_ref[pl.ds(n0, 16), :].astype(F32), 0.0)
            ext = jnp.concatenate([prev, cur, nxt], axis=0)
            z = cb + cw[0:1] * pltpu.roll(ext, 2, axis=0)[16:16 + rn]
            z = z + cw[1:2] * pltpu.roll(ext, 1, axis=0)[16:16 + rn]
            z = z + cw[2:3] * cur
            z = z + cw[3:4] * pltpu.roll(ext, rn + 31, axis=0)[16:16 + rn]
            zs_ref[pl.ds(r0, rn), :] = z
            return carry

        lax.fori_loop(0, nch, conv_body, 0)

        def coeffs(r0, d):
            z = zs_ref[pl.ds(r0, rn), :]
            e = jnp.exp2(jnp.dot(z.astype(BF16), wg_ref[d], preferred_element_type=F32) + bg_ref[d])
            r = 1.0 / (1.0 + e[:, :LANES])
            i = 1.0 / (1.0 + e[:, LANES:])
            a = jnp.exp2(nsp2[d] * r)
            return a, _sqrt_nonneg(1.0 - a * a) * (i * z)

        def gate(r0):
            return _silu(g_ref[pl.ds(r0, rn), :].astype(F32))

        if nch == 1:
            a, b = coeffs(0, 0)
            hf, carry_f = _scan_chunk(a, b, carry_f, False)
            a, b = coeffs(0, 1)
            hr, carry_r = _scan_chunk(a, b, carry_r, True)
            out_ref[...] = ((hf + hr) * gate(0)).astype(out_ref.dtype)
            return carry_f, carry_r

        def step(c, carry, combine):
            cf, cr = carry
            r0 = pl.multiple_of(c * rn, rn)
            r1 = pl.multiple_of((nch - 1 - c) * rn, rn)
            a, b = coeffs(r0, 0)
            hf, cf = _scan_chunk(a, b, cf, False)
            a, b = coeffs(r1, 1)
            hr, cr = _scan_chunk(a, b, cr, True)
            if combine:
                out_ref[pl.ds(r0, rn), :] = ((hf + hb_ref[pl.ds(r0, rn), :]) * gate(r0)).astype(out_ref.dtype)
                out_ref[pl.ds(r1, rn), :] = ((hb_ref[pl.ds(r1, rn), :] + hr) * gate(r1)).astype(out_ref.dtype)
            else:
                hb_ref[pl.ds(r0, rn), :] = hf
                hb_ref[pl.ds(r1, rn), :] = hr
            return cf, cr

        carry = lax.fori_loop(0, nch // 2, functools.partial(step, combine=False), (carry_f, carry_r),
                              unroll=LRU_UNROLL)
        return lax.fori_loop(nch // 2, nch, functools.partial(step, combine=True), carry,
                             unroll=LRU_UNROLL)

    zero = jnp.zeros((1, LANES), F32)
    cf, cr = run_seq(xca_ref, gca_ref, oc_ref, zero, zero)
    run_seq(xa_ref, ga_ref, o_ref, cf, cr)


def _lru_branch(u, uc, conv_w, conv_b, wg, bg, lam):
    b, t, _ = u.shape
    tc = uc.shape[1]
    w = conv_w.shape[1]
    nh = w // LANES
    ga0 = 2 * nh
    return pl.pallas_call(
        _lru_kernel,
        out_shape=(jax.ShapeDtypeStruct((b, t, w), BF16), jax.ShapeDtypeStruct((b, tc, w), BF16)),
        grid=(b, nh),
        in_specs=[
            pl.BlockSpec((None, t, LANES), lambda bi, h: (bi, 0, h)),
            pl.BlockSpec((None, t, LANES), lambda bi, h: (bi, 0, ga0 + h)),
            pl.BlockSpec((None, tc, LANES), lambda bi, h: (bi, 0, h)),
            pl.BlockSpec((None, tc, LANES), lambda bi, h: (bi, 0, ga0 + h)),
            pl.BlockSpec((conv_w.shape[0], LANES), lambda bi, h: (0, h)),
            pl.BlockSpec((1, LANES), lambda bi, h: (0, h)),
            pl.BlockSpec((2, None, LANES, 2 * LANES), lambda bi, h: (0, h, 0, 0)),
            pl.BlockSpec((2, None, 1, 2 * LANES), lambda bi, h: (0, h, 0, 0)),
            pl.BlockSpec((2, None, 1, LANES), lambda bi, h: (0, h, 0, 0)),
        ],
        out_specs=(pl.BlockSpec((None, t, LANES), lambda bi, h: (bi, 0, h)),
                   pl.BlockSpec((None, tc, LANES), lambda bi, h: (bi, 0, h))),
        scratch_shapes=[pltpu.VMEM((t, LANES), F32), pltpu.VMEM((t, LANES), F32)],
        compiler_params=_cparams(2),
        name="rglru",
    )(u, u, uc, uc, conv_w, conv_b, wg, bg, lam)


def _w12_kernel(cs_ref, wf_ref, o_ref):
    gd = wf_ref.shape[0]
    for half in range(gd // LANES):
        wf = wf_ref[:, half * LANES:(half + 1) * LANES]
        o_ref[half, :, :LANES] = jnp.dot(cs_ref[:gd, :], wf, preferred_element_type=F32,
                                         precision=lax.Precision.HIGHEST).astype(o_ref.dtype)
        o_ref[half, :, LANES:] = jnp.dot(cs_ref[gd:, :], wf, preferred_element_type=F32,
                                         precision=lax.Precision.HIGHEST).astype(o_ref.dtype)


def _fourier_weights(fnet_w):
    ne, ng, gd, _ = fnet_w.shape
    nhalf = gd // LANES
    idx = np.arange(gd)
    ang = 2.0 * np.pi * ((idx[:, None] * idx[None, :]) % gd) / gd
    cs = np.concatenate([np.cos(ang), np.sin(ang)], axis=0) / math.sqrt(gd)
    out = pl.pallas_call(
        _w12_kernel,
        out_shape=jax.ShapeDtypeStruct((ne, ng, nhalf, gd, 2 * LANES), BF16),
        grid=(ne, ng),
        in_specs=[pl.BlockSpec((2 * gd, gd), lambda e, g: (0, 0)),
                  pl.BlockSpec((None, None, gd, gd), lambda e, g: (e, g, 0, 0))],
        out_specs=pl.BlockSpec((None, None, nhalf, gd, 2 * LANES), lambda e, g: (e, g, 0, 0, 0)),
        compiler_params=_cparams(2),
        name="fourier_weights",
    )(jnp.asarray(cs, F32), fnet_w)
    return out.reshape(ne, ng * nhalf, gd, 2 * LANES)


def _chanmix_kernel(xb_ref, w_ref, ur_ref, ui_ref):
    ncb, gd, _ = w_ref.shape
    per_group = gd // LANES
    for cb in range(ncb):
        g = cb // per_group
        r = jnp.dot(xb_ref[:, g * gd:(g + 1) * gd], w_ref[cb], preferred_element_type=F32)
        ur_ref[:, cb * LANES:(cb + 1) * LANES] = r[:, :LANES]
        ui_ref[:, cb * LANES:(cb + 1) * LANES] = r[:, LANES:]


def _chanmix(u, w12):
    b, t, _ = u.shape
    ncb, gd, _ = w12.shape
    w = ncb * LANES
    tm = min(ROW_TILE, t)
    out = jax.ShapeDtypeStruct((b, t, w), F32)
    return pl.pallas_call(
        _chanmix_kernel,
        out_shape=(out, out),
        grid=(b, t // tm),
        in_specs=[pl.BlockSpec((None, tm, w), lambda bi, i: (bi, i, 1)),
                  pl.BlockSpec((ncb, gd, 2 * LANES), lambda bi, i: (0, 0, 0))],
        out_specs=(pl.BlockSpec((None, tm, w), lambda bi, i: (bi, i, 0)),
                   pl.BlockSpec((None, tm, w), lambda bi, i: (bi, i, 0))),
        compiler_params=_cparams(2),
        name="fourier_chanmix",
    )(u, w12)


def _dft_a_kernel(xb_ref, w_ref, ea_ref, twr_ref, twi_ref, br_ref, bi_ref, ur_s, ui_s):
    n2, nj, gd = xb_ref.shape
    u = jnp.dot(xb_ref[...].reshape(n2 * nj, gd), w_ref[...], preferred_element_type=F32)
    ur_s[...] = u[:, :LANES]
    ui_s[...] = u[:, LANES:]
    xr = jnp.concatenate([ur_s[pl.ds(j, n2, stride=nj), :] for j in range(nj)], axis=1)
    xi = jnp.concatenate([ui_s[pl.ds(j, n2, stride=nj), :] for j in range(nj)], axis=1)
    rhs = jnp.concatenate([xr, xi], axis=0).astype(BF16)
    res = jnp.dot(ea_ref[...], rhs, preferred_element_type=F32)
    for j in range(nj):
        pr = res[:n2, j * LANES:(j + 1) * LANES]
        pi = res[n2:, j * LANES:(j + 1) * LANES]
        tr = twr_ref[j]
        ti = twi_ref[j]
        br_ref[j] = (pr * tr - pi * ti).astype(br_ref.dtype)
        bi_ref[j] = (pr * ti + pi * tr).astype(bi_ref.dtype)


def _dft_b_kernel(br_ref, bi_ref, gb_ref, eb_ref, o_ref, sr_ref, si_ref):
    n1, nj, _ = br_ref.shape
    sr_ref[...] = br_ref[...].astype(F32).reshape(n1 * nj, LANES)
    si_ref[...] = bi_ref[...].astype(F32).reshape(n1 * nj, LANES)
    xr = jnp.concatenate([sr_ref[pl.ds(j, n1, stride=nj), :] for j in range(nj)], axis=1)
    xi = jnp.concatenate([si_ref[pl.ds(j, n1, stride=nj), :] for j in range(nj)], axis=1)
    rhs = jnp.concatenate([xr, xi], axis=0).astype(BF16)
    res = jnp.dot(eb_ref[...], rhs, preferred_element_type=F32)
    for j in range(nj):
        sr_ref[pl.ds(j, n1, stride=nj), :] = res[:, j * LANES:(j + 1) * LANES]
    fy = sr_ref[...].reshape(n1, nj, LANES)
    o_ref[...] = (fy * _silu(gb_ref[...].astype(F32))).astype(o_ref.dtype)


def _dft_consts(t):
    n2 = FFT_N2
    n1 = t // n2
    i2 = np.arange(n2)
    a2 = 2.0 * np.pi * ((i2[:, None] * i2[None, :]) % n2) / n2
    er, ei = np.cos(a2), np.sin(a2)
    ea = np.block([[er, -ei], [ei, er]])
    i1 = np.arange(n1)
    atw = 2.0 * np.pi * ((i1[:, None] * i2[None, :]) % t) / t
    a1 = 2.0 * np.pi * ((i1[:, None] * i1[None, :]) % n1) / n1
    eb = np.concatenate([np.cos(a1), -np.sin(a1)], axis=1) / math.sqrt(t)
    twr = jnp.broadcast_to(jnp.asarray(np.cos(atw), F32)[:, :, None], (n1, n2, LANES))
    twi = jnp.broadcast_to(jnp.asarray(np.sin(atw), F32)[:, :, None], (n1, n2, LANES))
    return jnp.asarray(ea, F32).astype(BF16), twr, twi, jnp.asarray(eb, F32).astype(BF16)


def _seq_fourier(u, w12, consts):
    b, t, n_in = u.shape
    ncb, gd, _ = w12.shape
    w = ncb * LANES
    n2 = FFT_N2
    n1 = t // n2
    nj = FFT_J
    ea, twr, twi, eb = consts
    u4 = u.reshape(b, n2, n1, n_in)
    xb0 = w // gd
    per_group = gd // LANES
    mid = jax.ShapeDtypeStruct((b, n1, n2, w), BF16)
    br, bi = pl.pallas_call(
        _dft_a_kernel,
        out_shape=(mid, mid),
        grid=(n1 // nj, b, ncb),
        in_specs=[
            pl.BlockSpec((None, n2, nj, gd), lambda tb, bi_, cb: (bi_, 0, tb, xb0 + cb // per_group)),
            pl.BlockSpec((None, gd, 2 * LANES), lambda tb, bi_, cb: (cb, 0, 0)),
            pl.BlockSpec((2 * n2, 2 * n2), lambda tb, bi_, cb: (0, 0)),
            pl.BlockSpec((nj, n2, LANES), lambda tb, bi_, cb: (tb, 0, 0)),
            pl.BlockSpec((nj, n2, LANES), lambda tb, bi_, cb: (tb, 0, 0)),
        ],
        out_specs=(pl.BlockSpec((None, nj, n2, LANES), lambda tb, bi_, cb: (bi_, tb, 0, cb)),
                   pl.BlockSpec((None, nj, n2, LANES), lambda tb, bi_, cb: (bi_, tb, 0, cb))),
        scratch_shapes=[pltpu.VMEM((n2 * nj, LANES), F32), pltpu.VMEM((n2 * nj, LANES), F32)],
        compiler_params=_cparams(3),
        name="dft_stage_a",
    )(u4, w12, ea, twr, twi)
    gb0 = (n_in - w) // LANES
    out = pl.pallas_call(
        _dft_b_kernel,
        out_shape=jax.ShapeDtypeStruct((b, n1, n2, w), BF16),
        grid=(b, n2 // nj, w // LANES),
        in_specs=[
            pl.BlockSpec((None, n1, nj, LANES), lambda bi_, kb, cb: (bi_, 0, kb, cb)),
            pl.BlockSpec((None, n1, nj, LANES), lambda bi_, kb, cb: (bi_, 0, kb, cb)),
            pl.BlockSpec((None, n1, nj, LANES), lambda bi_, kb, cb: (bi_, 0, kb, gb0 + cb)),
            pl.BlockSpec((n1, 2 * n1), lambda bi_, kb, cb: (0, 0)),
        ],
        out_specs=pl.BlockSpec((None, n1, nj, LANES), lambda bi_, kb, cb: (bi_, 0, kb, cb)),
        scratch_shapes=[pltpu.VMEM((n1 * nj, LANES), F32), pltpu.VMEM((n1 * nj, LANES), F32)],
        compiler_params=_cparams(3),
        name="dft_stage_b",
    )(br, bi, u.reshape(b, n1, n2, n_in), eb)
    return out.reshape(b, t, w)


def _dft_dense_kernel(ur_ref, ui_ref, gb_ref, ec_ref, o_ref):
    rhs = jnp.concatenate([ur_ref[...], ui_ref[...]], axis=0).astype(BF16)
    fy = jnp.dot(ec_ref[...], rhs, preferred_element_type=F32)
    o_ref[...] = (fy * _silu(gb_ref[...].astype(F32))).astype(o_ref.dtype)


def _seq_dft_dense(ur, ui, u):
    b, t, w = ur.shape
    gb_blk = u.shape[-1] // w - 1
    it = np.arange(t)
    ang = 2.0 * np.pi * ((it[:, None] * it[None, :]) % t) / t
    ec = jnp.asarray(np.concatenate([np.cos(ang), -np.sin(ang)], axis=1) / math.sqrt(t), F32).astype(BF16)
    return pl.pallas_call(
        _dft_dense_kernel,
        out_shape=jax.ShapeDtypeStruct((b, t, w), BF16),
        grid=(b,),
        in_specs=[pl.BlockSpec((None, t, w), lambda bi: (bi, 0, 0)),
                  pl.BlockSpec((None, t, w), lambda bi: (bi, 0, 0)),
                  pl.BlockSpec((None, t, w), lambda bi: (bi, 0, gb_blk)),
                  pl.BlockSpec((t, 2 * t), lambda bi: (0, 0))],
        out_specs=pl.BlockSpec((None, t, w), lambda bi: (bi, 0, 0)),
        compiler_params=_cparams(1),
        name="dft_dense",
    )(ur, ui, u, ec)


def _ret_state_kernel(lg_ref, k_ref, v_ref, s0_ref, sr_ref, se_ref, s_ref, kd_ref):
    nh, dk, dv = s_ref.shape
    cr = k_ref.shape[0]

    @pl.when(pl.program_id(1) == 0)
    def _init():
        s_ref[...] = s0_ref[...]
        rk = lax.broadcasted_iota(jnp.int32, (cr, dk), 0).astype(F32)
        for h in range(nh):
            kd_ref[h] = jnp.exp(lg_ref[1, h] * rk)

    for h in range(nh):
        cdec = jnp.exp(jnp.full((1, dv), lg_ref[1, h] * cr, F32))
        st = s_ref[h]
        sr_ref[h] = st.astype(sr_ref.dtype)
        kdec = (k_ref[:, h * dk:(h + 1) * dk].astype(F32) * kd_ref[h]).astype(BF16)
        s_ref[h] = cdec * st + lax.dot_general(kdec, v_ref[:, h * dv:(h + 1) * dv],
                                               (((0,), (0,)), ((), ())), preferred_element_type=F32)

    @pl.when(pl.program_id(1) == pl.num_programs(1) - 1)
    def _fin():
        se_ref[...] = s_ref[...]


def _ret_main_kernel(lg_ref, q_ref, k_ref, v_ref, sr_ref, s0_ref, o_ref, se_ref,
                     s_ref, dm_ref, qf_ref, qr_ref, kd_ref):
    nh, dk, dv = s_ref.shape
    cr = q_ref.shape[0]

    @pl.when(pl.program_id(1) == 0)
    def _init():
        s_ref[...] = s0_ref[...]
        ii = lax.broadcasted_iota(jnp.int32, (cr, cr), 0).astype(F32)
        jj = lax.broadcasted_iota(jnp.int32, (cr, cr), 1).astype(F32)
        diff = jj - ii
        past = diff <= 0
        rk = lax.broadcasted_iota(jnp.int32, (cr, dk), 0).astype(F32)
        for h in range(nh):
            lgf = lg_ref[0, h]
            lgr = lg_ref[1, h]
            dm_ref[h] = jnp.where(past, jnp.exp(lgf * jnp.where(past, -diff, 0.0)),
                                  jnp.exp(lgr * jnp.where(past, 0.0, diff)))
            qf_ref[h] = jnp.exp(lgf * (rk + 1.0))
            qr_ref[h] = jnp.exp(lgr * (cr - rk))
            kd_ref[h] = jnp.exp(lgf * ((cr - 1.0) - rk))

    for h in range(nh):
        cdec = jnp.exp(jnp.full((1, dv), lg_ref[0, h] * cr, F32))
        qh = q_ref[:, h * dk:(h + 1) * dk]
        kh = k_ref[:, h * dk:(h + 1) * dk]
        vh = v_ref[:, h * dv:(h + 1) * dv]
        s = lax.dot_general(qh, kh, (((1,), (1,)), ((), ())), preferred_element_type=F32)
        st = s_ref[h]
        qf = qh.astype(F32)
        lhs = jnp.concatenate([(s * dm_ref[h]).astype(BF16), (qf * qf_ref[h]).astype(BF16),
                               (qf * qr_ref[h]).astype(BF16)], axis=1)
        rhs = jnp.concatenate([vh, st.astype(BF16), sr_ref[h]], axis=0)
        o = jnp.dot(lhs, rhs, preferred_element_type=F32)
        kdec = (kh.astype(F32) * kd_ref[h]).astype(BF16)
        s_ref[h] = cdec * st + lax.dot_general(kdec, vh, (((0,), (0,)), ((), ())),
                                               preferred_element_type=F32)
        mu = jnp.mean(o, axis=-1, keepdims=True)
        oc = o - mu
        var = jnp.mean(oc * oc, axis=-1, keepdims=True)
        o_ref[:, h * dv:(h + 1) * dv] = (oc * lax.rsqrt(var + EPS)).astype(o_ref.dtype)

    @pl.when(pl.program_id(1) == pl.num_programs(1) - 1)
    def _fin():
        se_ref[...] = s_ref[...]


def _ret_seq(u, lg, s0_f, s0_r):
    b, t, _ = u.shape
    _, nh, dk, dv = s0_f.shape
    cr = min(RET_CHUNK, t)
    nc = t // cr
    qk = nh * dk
    vw = nh * dv
    v_blk = (2 * qk) // vw
    state = jax.ShapeDtypeStruct((b, nh, dk, dv), F32)
    state_spec = pl.BlockSpec((None, nh, dk, dv), lambda bi, i: (bi, 0, 0, 0))
    smem = pl.BlockSpec(memory_space=pltpu.SMEM)
    sr, se_r = pl.pallas_call(
        _ret_state_kernel,
        out_shape=(jax.ShapeDtypeStruct((b, nc, nh, dk, dv), BF16), state),
        grid=(b, nc),
        in_specs=[smem,
                  pl.BlockSpec((None, cr, qk), lambda bi, i: (bi, nc - 1 - i, 1)),
                  pl.BlockSpec((None, cr, vw), lambda bi, i: (bi, nc - 1 - i, v_blk)),
                  state_spec],
        out_specs=(pl.BlockSpec((None, None, nh, dk, dv), lambda bi, i: (bi, nc - 1 - i, 0, 0, 0)),
                   state_spec),
        scratch_shapes=[pltpu.VMEM((nh, dk, dv), F32), pltpu.VMEM((nh, cr, dk), F32)],
        compiler_params=_cparams(2),
        name="retention_state",
    )(lg, u, u, s0_r)
    o, se_f = pl.pallas_call(
        _ret_main_kernel,
        out_shape=(jax.ShapeDtypeStruct((b, t, vw), BF16), state),
        grid=(b, nc),
        in_specs=[smem,
                  pl.BlockSpec((None, cr, qk), lambda bi, i: (bi, i, 0)),
                  pl.BlockSpec((None, cr, qk), lambda bi, i: (bi, i, 1)),
                  pl.BlockSpec((None, cr, vw), lambda bi, i: (bi, i, v_blk)),
                  pl.BlockSpec((None, None, nh, dk, dv), lambda bi, i: (bi, i, 0, 0, 0)),
                  state_spec],
        out_specs=(pl.BlockSpec((None, cr, vw), lambda bi, i: (bi, i, 0)), state_spec),
        scratch_shapes=[
            pltpu.VMEM((nh, dk, dv), F32),
            pltpu.VMEM((nh, cr, cr), F32),
            pltpu.VMEM((nh, cr, dk), F32),
            pltpu.VMEM((nh, cr, dk), F32),
            pltpu.VMEM((nh, cr, dk), F32),
        ],
        compiler_params=_cparams(2),
        name="retention_main",
    )(lg, u, u, u, sr, s0_f)
    return o, se_f, se_r


def _retention(u, uc, lg):
    b = u.shape[0]
    nh = RET_HEADS
    dk = u.shape[-1] // (6 * nh)
    zero = jnp.zeros((b, nh, dk, 2 * dk), F32)
    mc, s_f, s_r = _ret_seq(uc, lg, zero, zero)
    m, _, _ = _ret_seq(u, lg, s_f, s_r)
    return m, mc


def _rope_tables(t):
    nf = LANES // 2
    inv = ROPE_THETA ** (-jnp.arange(nf, dtype=F32) / nf)
    ar = jnp.arange(t // GRID_W, dtype=F32)[:, None] * inv
    ac = jnp.arange(GRID_W, dtype=F32)[:, None] * inv
    dup = lambda v: jnp.concatenate([v, v], axis=1)
    sgn = lambda v: jnp.concatenate([-v, v], axis=1)
    return dup(jnp.cos(ar)), sgn(jnp.sin(ar)), dup(jnp.cos(ac)), sgn(jnp.sin(ac))


def kernel(x, c, ctx, c_ctx, mod_w, mod_b, pre_g, post_g, mix_w_in, mix_w_out, conv_w, conv_b,
           lru_wa, lru_ba, lru_wx, lru_bx, lru_lam, fnet_w, ret_w_in, ret_w_out, ret_log_gamma):
    b, t, d = x.shape
    tc = ctx.shape[1]
    depth = mod_w.shape[0]
    assert t % ROW_TILE == 0 and t % (FFT_N2 * FFT_J) == 0 and (t // FFT_N2) % FFT_J == 0
    assert tc % GRID_W == 0 and tc % 16 == 0 and tc <= ROW_TILE and b + 1 <= 8

    cond8 = jnp.zeros((8, d), F32).at[:b].set(c).at[b].set(c_ctx)
    mods = _ada_mod(cond8, mod_w, mod_b)

    w12 = _fourier_weights(fnet_w)
    dft_consts = _dft_consts(t)
    qk = ret_w_in.shape[-1] // 6
    rope_l = _rope_tables(t) + (2 * qk,)
    ones = lambda n: jnp.ones((n, LANES), F32)
    zeros = lambda n: jnp.zeros((n, LANES), F32)
    rope_c = (ones(tc // GRID_W), zeros(tc // GRID_W), ones(GRID_W), zeros(GRID_W), 2 * qk)

    nh_l = LRU_HEADS
    hd = lru_wa.shape[-1]
    wg_all = (jnp.concatenate([lru_wa, lru_wx], axis=-1) * (-LOG2E)).astype(BF16)
    bg_all = jnp.concatenate([lru_ba.reshape(-1, 2, nh_l, 1, hd),
                              lru_bx.reshape(-1, 2, nh_l, 1, hd)], axis=-1) * (-LOG2E)
    lam_all = lru_lam.reshape(-1, 2, nh_l, 1, hd)

    kscale = jnp.concatenate([jnp.ones((qk,), F32), jnp.full((qk,), (qk // RET_HEADS) ** -0.5, F32),
                              jnp.ones((4 * qk,), F32)])

    def w_in_of(layer):
        if layer % 2 == 0:
            return mix_w_in[layer // 2].astype(BF16)
        return (ret_w_in[layer // 2] * kscale).astype(BF16)

    def mod_of(layer, latent):
        m_l = mods[layer]
        parts = (m_l[:, :d], m_l[:, d:2 * d], m_l[:, 2 * d:])
        if latent:
            return tuple(v[:b].reshape(b, 1, d) for v in parts)
        return tuple(jnp.broadcast_to(v[b].reshape(1, 1, d), (b, 1, d)) for v in parts)

    def inp_of(layer, latent):
        shift, scale, _ = mod_of(layer, latent)
        rope = None if layer % 2 == 0 else (rope_l if latent else rope_c)
        return (pre_g[layer].reshape(1, d), scale, shift, w_in_of(layer), rope)

    xc = ctx
    _, u = _proj(x, inp=inp_of(0, True))
    _, uc = _proj(xc, inp=inp_of(0, False))
    for layer in range(depth):
        last = layer == depth - 1
        post = post_g[layer].reshape(1, d)
        gt_l = mod_of(layer, True)[2]
        gt_c = mod_of(layer, False)[2]
        if layer % 2 == 0:
            e = layer // 2
            w_out = mix_w_out[e].astype(BF16)
            ma, mca = _lru_branch(u, uc, conv_w[e], conv_b[e].reshape(1, -1),
                                  wg_all[e], bg_all[e], lam_all[e])
            mb = _seq_fourier(u, w12[e], dft_consts)
            out_l = ("even", ma, mb, w_out, post, gt_l)
            if not last:
                urc, uic = _chanmix(uc, w12[e])
                out_c = ("even", mca, _seq_dft_dense(urc, uic, uc), w_out, post, gt_c)
        else:
            j = layer // 2
            w_out = ret_w_out[j].astype(BF16)
            m, mc = _retention(u, uc, ret_log_gamma[j])
            out_l = ("odd", m, u, w_out, post, gt_l)
            out_c = ("odd", mc, uc, w_out, post, gt_c)
        if last:
            x, _ = _proj(x, out=out_l)
        else:
            x, u = _proj(x, out=out_l, inp=inp_of(layer + 1, True))
            xc, uc = _proj(xc, out=out_c, inp=inp_of(layer + 1, False))
    return x
```

```python
import functools
import math

import numpy as np
import jax
import jax.numpy as jnp
from jax import lax
from jax.experimental import pallas as pl
from jax.experimental.pallas import tpu as pltpu

F32 = jnp.float32
BF16 = jnp.bfloat16

EPS = 1e-6
LOG2E = math.log2(math.e)
LRU_C = 8.0
LRU_HEADS = 8
FNET_GROUPS = 4
RET_HEADS = 4
GRID_W = 64
ROPE_THETA = 10000.0

LANES = 128
VMEM_LIMIT = 56 * 1024 * 1024

ROW_TILE = 512
LRU_CHUNK = 256
LRU_UNROLL = 4
RET_CHUNK = 256
FFT_N2 = 128
FFT_J = 16


def _cparams(n_axes):
    return pltpu.CompilerParams(dimension_semantics=("arbitrary",) * n_axes,
                                vmem_limit_bytes=VMEM_LIMIT)


def _silu(v):
    return v * jax.nn.sigmoid(v)


def _ada_kernel(cond_ref, w_ref, b_ref, o_ref):
    s = _silu(cond_ref[...])
    o_ref[...] = jnp.dot(s, w_ref[...], preferred_element_type=F32,
                         precision=lax.Precision.HIGHEST) + b_ref[...]


def _ada_mod(cond8, mod_w, mod_b):
    depth, d, n3 = mod_w.shape
    tn = d
    return pl.pallas_call(
        _ada_kernel,
        out_shape=jax.ShapeDtypeStruct((depth, 8, n3), F32),
        grid=(depth, n3 // tn),
        in_specs=[
            pl.BlockSpec((8, d), lambda l, j: (0, 0)),
            pl.BlockSpec((None, d, tn), lambda l, j: (l, 0, j)),
            pl.BlockSpec((None, 1, tn), lambda l, j: (l, 0, j)),
        ],
        out_specs=pl.BlockSpec((None, 8, tn), lambda l, j: (l, 0, j)),
        compiler_params=_cparams(2),
        name="ada_mod",
    )(cond8, mod_w, mod_b.reshape(depth, 1, n3))


def _proj_kernel(*refs, out_kind, has_in, rope_cols, chunk):
    it = iter(refs)
    if out_kind:
        a_ref, b_ref = next(it), next(it)
    x_ref = next(it)
    if out_kind:
        wo_ref, pg_ref, gate_ref = next(it), next(it), next(it)
    if has_in:
        g_ref, sc_ref, sh_ref, wi_ref = next(it), next(it), next(it), next(it)
        if rope_cols:
            crow_ref, srow_ref, ccol_ref, scol_ref = next(it), next(it), next(it), next(it)
    if out_kind:
        xo_ref = next(it)
    if has_in:
        u_ref = next(it)

    x = x_ref[...]
    if out_kind == "even":
        wa = a_ref.shape[-1]
        y = jnp.dot(a_ref[...], wo_ref[:wa, :], preferred_element_type=F32)
        y = y + jnp.dot(b_ref[...], wo_ref[wa:, :], preferred_element_type=F32)
    elif out_kind == "odd":
        m = (a_ref[...].astype(F32) * _silu(b_ref[...].astype(F32))).astype(BF16)
        y = jnp.dot(m, wo_ref[...], preferred_element_type=F32)
    if out_kind:
        ms = jnp.mean(y * y, axis=-1, keepdims=True)
        x = x + gate_ref[...] * ((y * lax.rsqrt(ms + EPS)) * pg_ref[...])
        xo_ref[...] = x
    if not has_in:
        return

    ms = jnp.mean(x * x, axis=-1, keepdims=True)
    h = (x * lax.rsqrt(ms + EPS)) * g_ref[...]
    h = h * (1.0 + sc_ref[...]) + sh_ref[...]
    hb = h.astype(BF16)
    if rope_cols:
        nrow = crow_ref.shape[0]
        expand = lambda r: jnp.concatenate(
            [jnp.broadcast_to(r[g:g + 1, :], (GRID_W, LANES)) for g in range(nrow)], axis=0)
        repeat = lambda r: jnp.concatenate([r[...]] * nrow, axis=0)
        cos_t = (expand(crow_ref), repeat(ccol_ref))
        sin_t = (expand(srow_ref), repeat(scol_ref))
    n_out = u_ref.shape[-1]
    for c0 in range(0, n_out, chunk):
        r = jnp.dot(hb, wi_ref[:, c0:c0 + chunk], preferred_element_type=F32)
        if c0 < rope_cols:
            parts = []
            for s0 in range(0, chunk, LANES):
                axis = ((c0 + s0) // LANES) % 2
                ra = r[:, s0:s0 + LANES]
                parts.append(ra * cos_t[axis] + pltpu.roll(ra, LANES // 2, axis=1) * sin_t[axis])
            r = jnp.concatenate(parts, axis=1)
        u_ref[:, c0:c0 + chunk] = r.astype(u_ref.dtype)


def _proj(x, out=None, inp=None):
    b, t, d = x.shape
    tm = min(ROW_TILE, t)
    row = lambda bi, i: (bi, i, 0)
    const2 = lambda bi, i: (0, 0)
    per_b = lambda bi, i: (bi, 0, 0)
    in_specs, args, out_shape, out_specs = [], [], [], []
    out_kind = None
    if out is not None:
        out_kind, a, bb, w_out, pg, gate = out
        ka = a.shape[-1]
        in_specs.append(pl.BlockSpec((None, tm, ka), row))
        if out_kind == "even":
            in_specs.append(pl.BlockSpec((None, tm, bb.shape[-1]), row))
        else:
            g_blk = bb.shape[-1] // ka - 1
            in_specs.append(pl.BlockSpec((None, tm, ka), lambda bi, i: (bi, i, g_blk)))
        args += [a, bb]
    in_specs.append(pl.BlockSpec((None, tm, d), row))
    args.append(x)
    if out is not None:
        in_specs += [pl.BlockSpec(w_out.shape, const2, pipeline_mode=pl.Buffered(1)),
                     pl.BlockSpec((1, d), const2), pl.BlockSpec((None, 1, d), per_b)]
        args += [w_out, pg, gate]
        out_shape.append(jax.ShapeDtypeStruct((b, t, d), F32))
        out_specs.append(pl.BlockSpec((None, tm, d), row))
    rope_cols = 0
    if inp is not None:
        g, scale, shift, w_in, rope = inp
        n = w_in.shape[1]
        in_specs += [pl.BlockSpec((1, d), const2), pl.BlockSpec((None, 1, d), per_b),
                     pl.BlockSpec((None, 1, d), per_b),
                     pl.BlockSpec((d, n), const2, pipeline_mode=pl.Buffered(1))]
        args += [g, scale, shift, w_in]
        if rope is not None:
            crow, srow, ccol, scol, rope_cols = rope
            assert tm % GRID_W == 0
            nrow = tm // GRID_W
            in_specs += [pl.BlockSpec((nrow, LANES), lambda bi, i: (i, 0)),
                         pl.BlockSpec((nrow, LANES), lambda bi, i: (i, 0)),
                         pl.BlockSpec((GRID_W, LANES), const2), pl.BlockSpec((GRID_W, LANES), const2)]
            args += [crow, srow, ccol, scol]
        out_shape.append(jax.ShapeDtypeStruct((b, t, n), BF16))
        out_specs.append(pl.BlockSpec((None, tm, n), row))
    res = pl.pallas_call(
        functools.partial(_proj_kernel, out_kind=out_kind, has_in=inp is not None,
                          rope_cols=rope_cols, chunk=2 * LANES),
        out_shape=tuple(out_shape),
        grid=(b, t // tm),
        in_specs=in_specs,
        out_specs=tuple(out_specs),
        compiler_params=_cparams(2),
        name="proj_" + (out_kind or "first") + ("_in" if inp is not None else "_last"),
    )(*args)
    if out is None:
        return None, res[0]
    if inp is None:
        return res[0], None
    return res


def _group_scan(a, b, reverse):
    rows, lanes = a.shape
    ng = rows // 8
    a = a.reshape(ng, 8, lanes)
    b = b.reshape(ng, 8, lanes)
    sub = lax.broadcasted_iota(jnp.int32, (1, 8, lanes), 1)
    for dd in (1, 2, 4):
        if reverse:
            m = sub <= 7 - dd
            sh = 8 - dd
        else:
            m = sub >= dd
            sh = dd
        a_sh = pltpu.roll(a, sh, axis=1)
        b_sh = pltpu.roll(b, sh, axis=1)
        b = b + a * jnp.where(m, b_sh, 0.0)
        a = a * jnp.where(m, a_sh, 1.0)
    return a.reshape(rows, lanes), b.reshape(rows, lanes)


def _carry_chain(a, b, carry, reverse):
    ng = a.shape[0] // 8
    outs = [None] * ng
    order = range(ng - 1, -1, -1) if reverse else range(ng)
    for gi in order:
        hg = b[8 * gi:8 * gi + 8] + a[8 * gi:8 * gi + 8] * carry
        carry = hg[0:1] if reverse else hg[7:8]
        outs[gi] = hg
    return jnp.concatenate(outs, axis=0), carry


def _scan_chunk(a, b, carry, reverse):
    a, b = _group_scan(a, b, reverse)
    return _carry_chain(a, b, carry, reverse)


def _sqrt_nonneg(v):
    return jnp.where(v > 0.0, v * lax.rsqrt(v), 0.0)


def _lru_kernel(xa_ref, ga_ref, xca_ref, gca_ref, cw_ref, cb_ref, wg_ref, bg_ref, lam_ref,
                o_ref, oc_ref, zs_ref, hb_ref):
    cw = cw_ref[...]
    cb = cb_ref[...]
    nsp2 = []
    for d in (0, 1):
        y = -lam_ref[d]
        nsp2.append((-LRU_C * LOG2E) * (jnp.maximum(y, 0.0) + jnp.log1p(jnp.exp(-jnp.abs(y)))))

    def run_seq(x_ref, g_ref, out_ref, carry_f, carry_r):
        tn = x_ref.shape[0]
        rn = min(LRU_CHUNK, tn)
        nch = tn // rn
        assert nch == 1 or nch % 2 == 0

        def conv_body(c, carry):
            r0 = pl.multiple_of(c * rn, rn)
            p0 = pl.multiple_of(jnp.maximum(r0 - 16, 0), 16)
            n0 = pl.multiple_of(jnp.minimum(r0 + rn, tn - 16), 16)
            cur = x_ref[pl.ds(r0, rn), :].astype(F32)
            prev = jnp.where(c > 0, x_ref[pl.ds(p0, 16), :].astype(F32), 0.0)
            nxt = jnp.where(c < nch - 1, x_ref[pl.ds(n0, 16), :].astype(F32), 0.0)
            ext = jnp.concatenate([prev, cur, nxt], axis=0)
            z = cb + cw[0:1] * pltpu.roll(ext, 2, axis=0)[16:16 + rn]
            z = z + cw[1:2] * pltpu.roll(ext, 1, axis=0)[16:16 + rn]
            z = z + cw[2:3] * cur
            z = z + cw[3:4] * pltpu.roll(ext, rn + 31, axis=0)[16:16 + rn]
            zs_ref[pl.ds(r0, rn), :] = z
            return carry

        lax.fori_loop(0, nch, conv_body, 0)

        def coeffs(r0, d):
            z = zs_ref[pl.ds(r0, rn), :]
            e = jnp.exp2(jnp.dot(z.astype(BF16), wg_ref[d], preferred_element_type=F32) + bg_ref[d])
            r = 1.0 / (1.0 + e[:, :LANES])
            i = 1.0 / (1.0 + e[:, LANES:])
            a = jnp.exp2(nsp2[d] * r)
            return a, _sqrt_nonneg(1.0 - a * a) * (i * z)

        def gate(r0):
            return _silu(g_ref[pl.ds(r0, rn), :].astype(F32))

        if nch == 1:
            a, b = coeffs(0, 0)
            hf, carry_f = _scan_chunk(a, b, carry_f, False)
            a, b = coeffs(0, 1)
            hr, carry_r = _scan_chunk(a, b, carry_r, True)
            out_ref[...] = ((hf + hr) * gate(0)).astype(out_ref.dtype)
            return carry_f, carry_r

        def step(c, carry, combine):
            cf, cr = carry
            r0 = pl.multiple_of(c * rn, rn)
            r1 = pl.multiple_of((nch - 1 - c) * rn, rn)
            a, b = coeffs(r0, 0)
            hf, cf = _scan_chunk(a, b, cf, False)
            a, b = coeffs(r1, 1)
            hr, cr = _scan_chunk(a, b, cr, True)
            if combine:
                out_ref[pl.ds(r0, rn), :] = ((hf + hb_ref[pl.ds(r0, rn), :]) * gate(r0)).astype(out_ref.dtype)
                out_ref[pl.ds(r1, rn), :] = ((hb_ref[pl.ds(r1, rn), :] + hr) * gate(r1)).astype(out_ref.dtype)
            else:
                hb_ref[pl.ds(r0, rn), :] = hf
                hb_ref[pl.ds(r1, rn), :] = hr
            return cf, cr

        carry = lax.fori_loop(0, nch // 2, functools.partial(step, combine=False), (carry_f, carry_r),
                              unroll=LRU_UNROLL)
        return lax.fori_loop(nch // 2, nch, functools.partial(step, combine=True), carry,
                             unroll=LRU_UNROLL)

    zero = jnp.zeros((1, LANES), F32)
    cf, cr = run_seq(xca_ref, gca_ref, oc_ref, zero, zero)
    run_seq(xa_ref, ga_ref, o_ref, cf, cr)


def _lru_branch(u, uc, conv_w, conv_b, wg, bg, lam):
    b, t, _ = u.shape
    tc = uc.shape[1]
    w = conv_w.shape[1]
    nh = w // LANES
    ga0 = 2 * nh
    return pl.pallas_call(
        _lru_kernel,
        out_shape=(jax.ShapeDtypeStruct((b, t, w), BF16), jax.ShapeDtypeStruct((b, tc, w), BF16)),
        grid=(b, nh),
        in_specs=[
            pl.BlockSpec((None, t, LANES), lambda bi, h: (bi, 0, h)),
            pl.BlockSpec((None, t, LANES), lambda bi, h: (bi, 0, ga0 + h)),
            pl.BlockSpec((None, tc, LANES), lambda bi, h: (bi, 0, h)),
            pl.BlockSpec((None, tc, LANES), lambda bi, h: (bi, 0, ga0 + h)),
            pl.BlockSpec((conv_w.shape[0], LANES), lambda bi, h: (0, h)),
            pl.BlockSpec((1, LANES), lambda bi, h: (0, h)),
            pl.BlockSpec((2, None, LANES, 2 * LANES), lambda bi, h: (0, h, 0, 0)),
            pl.BlockSpec((2, None, 1, 2 * LANES), lambda bi, h: (0, h, 0, 0)),
            pl.BlockSpec((2, None, 1, LANES), lambda bi, h: (0, h, 0, 0)),
        ],
        out_specs=(pl.BlockSpec((None, t, LANES), lambda bi, h: (bi, 0, h)),
                   pl.BlockSpec((None, tc, LANES), lambda bi, h: (bi, 0, h))),
        scratch_shapes=[pltpu.VMEM((t, LANES), F32), pltpu.VMEM((t, LANES), F32)],
        compiler_params=_cparams(2),
        name="rglru",
    )(u, u, uc, uc, conv_w, conv_b, wg, bg, lam)


def _w12_kernel(cs_ref, wf_ref, o_ref):
    gd = wf_ref.shape[0]
    for half in range(gd // LANES):
        wf = wf_ref[:, half * LANES:(half + 1) * LANES]
        o_ref[half, :, :LANES] = jnp.dot(cs_ref[:gd, :], wf, preferred_element_type=F32,
                                         precision=lax.Precision.HIGHEST).astype(o_ref.dtype)
        o_ref[half, :, LANES:] = jnp.dot(cs_ref[gd:, :], wf, preferred_element_type=F32,
                                         precision=lax.Precision.HIGHEST).astype(o_ref.dtype)


def _fourier_weights(fnet_w):
    ne, ng, gd, _ = fnet_w.shape
    nhalf = gd // LANES
    idx = np.arange(gd)
    ang = 2.0 * np.pi * ((idx[:, None] * idx[None, :]) % gd) / gd
    cs = np.concatenate([np.cos(ang), np.sin(ang)], axis=0) / math.sqrt(gd)
    out = pl.pallas_call(
        _w12_kernel,
        out_shape=jax.ShapeDtypeStruct((ne, ng, nhalf, gd, 2 * LANES), BF16),
        grid=(ne, ng),
        in_specs=[pl.BlockSpec((2 * gd, gd), lambda e, g: (0, 0)),
                  pl.BlockSpec((None, None, gd, gd), lambda e, g: (e, g, 0, 0))],
        out_specs=pl.BlockSpec((None, None, nhalf, gd, 2 * LANES), lambda e, g: (e, g, 0, 0, 0)),
        compiler_params=_cparams(2),
        name="fourier_weights",
    )(jnp.asarray(cs, F32), fnet_w)
    return out.reshape(ne, ng * nhalf, gd, 2 * LANES)


def _chanmix_kernel(xb_ref, w_ref, ur_ref, ui_ref):
    ncb, gd, _ = w_ref.shape
    per_group = gd // LANES
    for cb in range(ncb):
        g = cb // per_group
        r = jnp.dot(xb_ref[:, g * gd:(g + 1) * gd], w_ref[cb], preferred_element_type=F32)
        ur_ref[:, cb * LANES:(cb + 1) * LANES] = r[:, :LANES]
        ui_ref[:, cb * LANES:(cb + 1) * LANES] = r[:, LANES:]


def _chanmix(u, w12):
    b, t, _ = u.shape
    ncb, gd, _ = w12.shape
    w = ncb * LANES
    tm = min(ROW_TILE, t)
    out = jax.ShapeDtypeStruct((b, t, w), F32)
    return pl.pallas_call(
        _chanmix_kernel,
        out_shape=(out, out),
        grid=(b, t // tm),
        in_specs=[pl.BlockSpec((None, tm, w), lambda bi, i: (bi, i, 1)),
                  pl.BlockSpec((ncb, gd, 2 * LANES), lambda bi, i: (0, 0, 0))],
        out_specs=(pl.BlockSpec((None, tm, w), lambda bi, i: (bi, i, 0)),
                   pl.BlockSpec((None, tm, w), lambda bi, i: (bi, i, 0))),
        compiler_params=_cparams(2),
        name="fourier_chanmix",
    )(u, w12)


def _dft_a_kernel(xb_ref, w_ref, ea_ref, twr_ref, twi_ref, br_ref, bi_ref, u_s):
    n2, nj, gd = xb_ref.shape
    ncol = w_ref.shape[0]
    xb = xb_ref[...].reshape(n2 * nj, gd)
    for c in range(ncol):
        u = jnp.dot(xb, w_ref[c], preferred_element_type=F32)
        u_s[2 * c] = u[:, :LANES]
        u_s[2 * c + 1] = u[:, LANES:]
    for c in range(ncol):
        ur_s, ui_s = u_s.at[2 * c], u_s.at[2 * c + 1]
        xr = jnp.concatenate([ur_s[pl.ds(j, n2, stride=nj), :] for j in range(nj)], axis=1)
        xi = jnp.concatenate([ui_s[pl.ds(j, n2, stride=nj), :] for j in range(nj)], axis=1)
        rhs = jnp.concatenate([xr, xi], axis=0).astype(BF16)
        res = jnp.dot(ea_ref[...], rhs, preferred_element_type=F32)
        cols = slice(c * LANES, (c + 1) * LANES)
        for j in range(nj):
            pr = res[:n2, j * LANES:(j + 1) * LANES]
            pi = res[n2:, j * LANES:(j + 1) * LANES]
            tr = twr_ref[j]
            ti = twi_ref[j]
            br_ref[j, :, cols] = (pr * tr - pi * ti).astype(br_ref.dtype)
            bi_ref[j, :, cols] = (pr * ti + pi * tr).astype(bi_ref.dtype)


def _dft_b_kernel(br_ref, bi_ref, gb_ref, eb_ref, o_ref, s_ref):
    n1, nj, width = br_ref.shape
    ncol = width // LANES
    for c in range(ncol):
        cols = slice(c * LANES, (c + 1) * LANES)
        sr_ref, si_ref, so_ref = s_ref.at[3 * c], s_ref.at[3 * c + 1], s_ref.at[3 * c + 2]
        sr_ref[...] = br_ref[:, :, cols].astype(F32).reshape(n1 * nj, LANES)
        si_ref[...] = bi_ref[:, :, cols].astype(F32).reshape(n1 * nj, LANES)
        xr = jnp.concatenate([sr_ref[pl.ds(j, n1, stride=nj), :] for j in range(nj)], axis=1)
        xi = jnp.concatenate([si_ref[pl.ds(j, n1, stride=nj), :] for j in range(nj)], axis=1)
        rhs = jnp.concatenate([xr, xi], axis=0).astype(BF16)
        res = jnp.dot(eb_ref[...], rhs, preferred_element_type=F32)
        for j in range(nj):
            so_ref[pl.ds(j, n1, stride=nj), :] = res[:, j * LANES:(j + 1) * LANES]
        fy = so_ref[...].reshape(n1, nj, LANES)
        o_ref[:, :, cols] = (fy * _silu(gb_ref[:, :, cols].astype(F32))).astype(o_ref.dtype)


def _dft_consts(t):
    n2 = FFT_N2
    n1 = t // n2
    i2 = np.arange(n2)
    a2 = 2.0 * np.pi * ((i2[:, None] * i2[None, :]) % n2) / n2
    er, ei = np.cos(a2), np.sin(a2)
    ea = np.block([[er, -ei], [ei, er]])
    i1 = np.arange(n1)
    atw = 2.0 * np.pi * ((i1[:, None] * i2[None, :]) % t) / t
    a1 = 2.0 * np.pi * ((i1[:, None] * i1[None, :]) % n1) / n1
    eb = np.concatenate([np.cos(a1), -np.sin(a1)], axis=1) / math.sqrt(t)
    twr = jnp.broadcast_to(jnp.asarray(np.cos(atw), F32)[:, :, None], (n1, n2, LANES))
    twi = jnp.broadcast_to(jnp.asarray(np.sin(atw), F32)[:, :, None], (n1, n2, LANES))
    return jnp.asarray(ea, F32).astype(BF16), twr, twi, jnp.asarray(eb, F32).astype(BF16)


def _seq_fourier(u, w12, consts):
    b, t, n_in = u.shape
    ncb, gd, _ = w12.shape
    w = ncb * LANES
    n2 = FFT_N2
    n1 = t // n2
    nj = FFT_J
    ea, twr, twi, eb = consts
    u4 = u.reshape(b, n2, n1, n_in)
    xb0 = w // gd
    per_group = gd // LANES
    mid = jax.ShapeDtypeStruct((b, n1, n2, w), BF16)
    br, bi = pl.pallas_call(
        _dft_a_kernel,
        out_shape=(mid, mid),
        grid=(n1 // nj, b, ncb // per_group),
        in_specs=[
            pl.BlockSpec((None, n2, nj, gd), lambda tb, bi_, g: (bi_, 0, tb, xb0 + g)),
            pl.BlockSpec((per_group, gd, 2 * LANES), lambda tb, bi_, g: (g, 0, 0)),
            pl.BlockSpec((2 * n2, 2 * n2), lambda tb, bi_, g: (0, 0)),
            pl.BlockSpec((nj, n2, LANES), lambda tb, bi_, g: (tb, 0, 0)),
            pl.BlockSpec((nj, n2, LANES), lambda tb, bi_, g: (tb, 0, 0)),
        ],
        out_specs=(pl.BlockSpec((None, nj, n2, gd), lambda tb, bi_, g: (bi_, tb, 0, g)),
                   pl.BlockSpec((None, nj, n2, gd), lambda tb, bi_, g: (bi_, tb, 0, g))),
        scratch_shapes=[pltpu.VMEM((2 * per_group, n2 * nj, LANES), F32)],
        compiler_params=_cparams(3),
        name="dft_stage_a",
    )(u4, w12, ea, twr, twi)
    gb0 = (n_in - w) // gd
    out = pl.pallas_call(
        _dft_b_kernel,
        out_shape=jax.ShapeDtypeStruct((b, n1, n2, w), BF16),
        grid=(b, n2 // nj, w // gd),
        in_specs=[
            pl.BlockSpec((None, n1, nj, gd), lambda bi_, kb, g: (bi_, 0, kb, g)),
            pl.BlockSpec((None, n1, nj, gd), lambda bi_, kb, g: (bi_, 0, kb, g)),
            pl.BlockSpec((None, n1, nj, gd), lambda bi_, kb, g: (bi_, 0, kb, gb0 + g)),
            pl.BlockSpec((n1, 2 * n1), lambda bi_, kb, g: (0, 0)),
        ],
        out_specs=pl.BlockSpec((None, n1, nj, gd), lambda bi_, kb, g: (bi_, 0, kb, g)),
        scratch_shapes=[pltpu.VMEM((3 * per_group, n1 * nj, LANES), F32)],
        compiler_params=_cparams(3),
        name="dft_stage_b",
    )(br, bi, u.reshape(b, n1, n2, n_in), eb)
    return out.reshape(b, t, w)


def _dft_dense_kernel(ur_ref, ui_ref, gb_ref, ec_ref, o_ref):
    rhs = jnp.concatenate([ur_ref[...], ui_ref[...]], axis=0).astype(BF16)
    fy = jnp.dot(ec_ref[...], rhs, preferred_element_type=F32)
    o_ref[...] = (fy * _silu(gb_ref[...].astype(F32))).astype(o_ref.dtype)


def _seq_dft_dense(ur, ui, u):
    b, t, w = ur.shape
    gb_blk = u.shape[-1] // w - 1
    it = np.arange(t)
    ang = 2.0 * np.pi * ((it[:, None] * it[None, :]) % t) / t
    ec = jnp.asarray(np.concatenate([np.cos(ang), -np.sin(ang)], axis=1) / math.sqrt(t), F32).astype(BF16)
    return pl.pallas_call(
        _dft_dense_kernel,
        out_shape=jax.ShapeDtypeStruct((b, t, w), BF16),
        grid=(b,),
        in_specs=[pl.BlockSpec((None, t, w), lambda bi: (bi, 0, 0)),
                  pl.BlockSpec((None, t, w), lambda bi: (bi, 0, 0)),
                  pl.BlockSpec((None, t, w), lambda bi: (bi, 0, gb_blk)),
                  pl.BlockSpec((t, 2 * t), lambda bi: (0, 0))],
        out_specs=pl.BlockSpec((None, t, w), lambda bi: (bi, 0, 0)),
        compiler_params=_cparams(1),
        name="dft_dense",
    )(ur, ui, u, ec)


def _ret_state_kernel(lg_ref, k_ref, v_ref, s0_ref, sr_ref, se_ref, s_ref, kd_ref):
    nh, dk, dv = s_ref.shape
    nsub = sr_ref.shape[0]
    cr = k_ref.shape[0] // nsub

    @pl.when(pl.program_id(1) == 0)
    def _init():
        s_ref[...] = s0_ref[...]
        rk = lax.broadcasted_iota(jnp.int32, (cr, dk), 0).astype(F32)
        for h in range(nh):
            kd_ref[h] = jnp.exp(lg_ref[1, h] * rk)

    for sub in range(nsub - 1, -1, -1):
        rows = slice(sub * cr, (sub + 1) * cr)
        for h in range(nh):
            cdec = jnp.exp(jnp.full((1, dv), lg_ref[1, h] * cr, F32))
            st = s_ref[h]
            sr_ref[sub, h] = st.astype(sr_ref.dtype)
            kdec = (k_ref[rows, h * dk:(h + 1) * dk].astype(F32) * kd_ref[h]).astype(BF16)
            s_ref[h] = cdec * st + lax.dot_general(kdec, v_ref[rows, h * dv:(h + 1) * dv],
                                                   (((0,), (0,)), ((), ())), preferred_element_type=F32)

    @pl.when(pl.program_id(1) == pl.num_programs(1) - 1)
    def _fin():
        se_ref[...] = s_ref[...]


def _ret_main_kernel(lg_ref, q_ref, k_ref, v_ref, sr_ref, s0_ref, o_ref, se_ref,
                     s_ref, dm_ref, qf_ref, qr_ref, kd_ref):
    nh, dk, dv = s_ref.shape
    cr = q_ref.shape[0]

    @pl.when(pl.program_id(1) == 0)
    def _init():
        s_ref[...] = s0_ref[...]
        ii = lax.broadcasted_iota(jnp.int32, (cr, cr), 0).astype(F32)
        jj = lax.broadcasted_iota(jnp.int32, (cr, cr), 1).astype(F32)
        diff = jj - ii
        past = diff <= 0
        rk = lax.broadcasted_iota(jnp.int32, (cr, dk), 0).astype(F32)
        for h in range(nh):
            lgf = lg_ref[0, h]
            lgr = lg_ref[1, h]
            dm_ref[h] = jnp.where(past, jnp.exp(lgf * jnp.where(past, -diff, 0.0)),
                                  jnp.exp(lgr * jnp.where(past, 0.0, diff)))
            qf_ref[h] = jnp.exp(lgf * (rk + 1.0))
            qr_ref[h] = jnp.exp(lgr * (cr - rk))
            kd_ref[h] = jnp.exp(lgf * ((cr - 1.0) - rk))

    for h in range(nh):
        cdec = jnp.exp(jnp.full((1, dv), lg_ref[0, h] * cr, F32))
        qh = q_ref[:, h * dk:(h + 1) * dk]
        kh = k_ref[:, h * dk:(h + 1) * dk]
        vh = v_ref[:, h * dv:(h + 1) * dv]
        s = lax.dot_general(qh, kh, (((1,), (1,)), ((), ())), preferred_element_type=F32)
        st = s_ref[h]
        qf = qh.astype(F32)
        lhs = jnp.concatenate([(s * dm_ref[h]).astype(BF16), (qf * qf_ref[h]).astype(BF16),
                               (qf * qr_ref[h]).astype(BF16)], axis=1)
        rhs = jnp.concatenate([vh, st.astype(BF16), sr_ref[h]], axis=0)
        o = jnp.dot(lhs, rhs, preferred_element_type=F32)
        kdec = (kh.astype(F32) * kd_ref[h]).astype(BF16)
        s_ref[h] = cdec * st + lax.dot_general(kdec, vh, (((0,), (0,)), ((), ())),
                                               preferred_element_type=F32)
        mu = jnp.mean(o, axis=-1, keepdims=True)
        oc = o - mu
        var = jnp.mean(oc * oc, axis=-1, keepdims=True)
        o_ref[:, h * dv:(h + 1) * dv] = (oc * lax.rsqrt(var + EPS)).astype(o_ref.dtype)

    @pl.when(pl.program_id(1) == pl.num_programs(1) - 1)
    def _fin():
        se_ref[...] = s_ref[...]


def _ret_seq(u, lg, s0_f, s0_r):
    b, t, _ = u.shape
    _, nh, dk, dv = s0_f.shape
    cr = min(RET_CHUNK, t)
    nc = t // cr
    qk = nh * dk
    vw = nh * dv
    v_blk = (2 * qk) // vw
    state = jax.ShapeDtypeStruct((b, nh, dk, dv), F32)
    state_spec = pl.BlockSpec((None, nh, dk, dv), lambda bi, i: (bi, 0, 0, 0))
    smem = pl.BlockSpec(memory_space=pltpu.SMEM)
    nsub = 2 if nc % 2 == 0 else 1
    ns = nc // nsub
    sr, se_r = pl.pallas_call(
        _ret_state_kernel,
        out_shape=(jax.ShapeDtypeStruct((b, nc, nh, dk, dv), BF16), state),
        grid=(b, ns),
        in_specs=[smem,
                  pl.BlockSpec((None, nsub * cr, qk), lambda bi, i: (bi, ns - 1 - i, 1)),
                  pl.BlockSpec((None, nsub * cr, vw), lambda bi, i: (bi, ns - 1 - i, v_blk)),
                  state_spec],
        out_specs=(pl.BlockSpec((None, nsub, nh, dk, dv), lambda bi, i: (bi, ns - 1 - i, 0, 0, 0)),
                   state_spec),
        scratch_shapes=[pltpu.VMEM((nh, dk, dv), F32), pltpu.VMEM((nh, cr, dk), F32)],
        compiler_params=_cparams(2),
        name="retention_state",
    )(lg, u, u, s0_r)
    o, se_f = pl.pallas_call(
        _ret_main_kernel,
        out_shape=(jax.ShapeDtypeStruct((b, t, vw), BF16), state),
        grid=(b, nc),
        in_specs=[smem,
                  pl.BlockSpec((None, cr, qk), lambda bi, i: (bi, i, 0)),
                  pl.BlockSpec((None, cr, qk), lambda bi, i: (bi, i, 1)),
                  pl.BlockSpec((None, cr, vw), lambda bi, i: (bi, i, v_blk)),
                  pl.BlockSpec((None, None, nh, dk, dv), lambda bi, i: (bi, i, 0, 0, 0)),
                  state_spec],
        out_specs=(pl.BlockSpec((None, cr, vw), lambda bi, i: (bi, i, 0)), state_spec),
        scratch_shapes=[
            pltpu.VMEM((nh, dk, dv), F32),
            pltpu.VMEM((nh, cr, cr), F32),
            pltpu.VMEM((nh, cr, dk), F32),
            pltpu.VMEM((nh, cr, dk), F32),
            pltpu.VMEM((nh, cr, dk), F32),
        ],
        compiler_params=_cparams(2),
        name="retention_main",
    )(lg, u, u, u, sr, s0_f)
    return o, se_f, se_r


def _retention(u, uc, lg):
    b = u.shape[0]
    nh = RET_HEADS
    dk = u.shape[-1] // (6 * nh)
    zero = jnp.zeros((b, nh, dk, 2 * dk), F32)
    mc, s_f, s_r = _ret_seq(uc, lg, zero, zero)
    m, _, _ = _ret_seq(u, lg, s_f, s_r)
    return m, mc


def _rope_tables(t):
    nf = LANES // 2
    inv = ROPE_THETA ** (-jnp.arange(nf, dtype=F32) / nf)
    ar = jnp.arange(t // GRID_W, dtype=F32)[:, None] * inv
    ac = jnp.arange(GRID_W, dtype=F32)[:, None] * inv
    dup = lambda v: jnp.concatenate([v, v], axis=1)
    sgn = lambda v: jnp.concatenate([-v, v], axis=1)
    return dup(jnp.cos(ar)), sgn(jnp.sin(ar)), dup(jnp.cos(ac)), sgn(jnp.sin(ac))


def kernel(x, c, ctx, c_ctx, mod_w, mod_b, pre_g, post_g, mix_w_in, mix_w_out, conv_w, conv_b,
           lru_wa, lru_ba, lru_wx, lru_bx, lru_lam, fnet_w, ret_w_in, ret_w_out, ret_log_gamma):
    b, t, d = x.shape
    tc = ctx.shape[1]
    depth = mod_w.shape[0]
    assert t % ROW_TILE == 0 and t % (FFT_N2 * FFT_J) == 0 and (t // FFT_N2) % FFT_J == 0
    assert tc % GRID_W == 0 and tc % 16 == 0 and tc <= ROW_TILE and b + 1 <= 8

    cond8 = jnp.zeros((8, d), F32).at[:b].set(c).at[b].set(c_ctx)
    mods = _ada_mod(cond8, mod_w, mod_b)

    w12 = _fourier_weights(fnet_w)
    dft_consts = _dft_consts(t)
    qk = ret_w_in.shape[-1] // 6
    rope_l = _rope_tables(t) + (2 * qk,)
    ones = lambda n: jnp.ones((n, LANES), F32)
    zeros = lambda n: jnp.zeros((n, LANES), F32)
    rope_c = (ones(tc // GRID_W), zeros(tc // GRID_W), ones(GRID_W), zeros(GRID_W), 2 * qk)

    nh_l = LRU_HEADS
    hd = lru_wa.shape[-1]
    wg_all = (jnp.concatenate([lru_wa, lru_wx], axis=-1) * (-LOG2E)).astype(BF16)
    bg_all = jnp.concatenate([lru_ba.reshape(-1, 2, nh_l, 1, hd),
                              lru_bx.reshape(-1, 2, nh_l, 1, hd)], axis=-1) * (-LOG2E)
    lam_all = lru_lam.reshape(-1, 2, nh_l, 1, hd)

    kscale = jnp.concatenate([jnp.ones((qk,), F32), jnp.full((qk,), (qk // RET_HEADS) ** -0.5, F32),
                              jnp.ones((4 * qk,), F32)])

    def w_in_of(layer):
        if layer % 2 == 0:
            return mix_w_in[layer // 2].astype(BF16)
        return (ret_w_in[layer // 2] * kscale).astype(BF16)

    def mod_of(layer, latent):
        m_l = mods[layer]
        parts = (m_l[:, :d], m_l[:, d:2 * d], m_l[:, 2 * d:])
        if latent:
            return tuple(v[:b].reshape(b, 1, d) for v in parts)
        return tuple(jnp.broadcast_to(v[b].reshape(1, 1, d), (b, 1, d)) for v in parts)

    def inp_of(layer, latent):
        shift, scale, _ = mod_of(layer, latent)
        rope = None if layer % 2 == 0 else (rope_l if latent else rope_c)
        return (pre_g[layer].reshape(1, d), scale, shift, w_in_of(layer), rope)

    xc = ctx
    _, u = _proj(x, inp=inp_of(0, True))
    _, uc = _proj(xc, inp=inp_of(0, False))
    for layer in range(depth):
        last = layer == depth - 1
        post = post_g[layer].reshape(1, d)
        gt_l = mod_of(layer, True)[2]
        gt_c = mod_of(layer, False)[2]
        if layer % 2 == 0:
            e = layer // 2
            w_out = mix_w_out[e].astype(BF16)
            ma, mca = _lru_branch(u, uc, conv_w[e], conv_b[e].reshape(1, -1),
                                  wg_all[e], bg_all[e], lam_all[e])
            mb = _seq_fourier(u, w12[e], dft_consts)
            out_l = ("even", ma, mb, w_out, post, gt_l)
            if not last:
                urc, uic = _chanmix(uc, w12[e])
                out_c = ("even", mca, _seq_dft_dense(urc, uic, uc), w_out, post, gt_c)
        else:
            j = layer // 2
            w_out = ret_w_out[j].astype(BF16)
            m, mc = _retention(u, uc, ret_log_gamma[j])
            out_l = ("odd", m, u, w_out, post, gt_l)
            out_c = ("odd", mc, uc, w_out, post, gt_c)
        if last:
            x, _ = _proj(x, out=out_l)
        else:
            x, u = _proj(x, out=out_l, inp=inp_of(layer + 1, True))
            xc, uc = _proj(xc, out=out_c, inp=inp_of(layer + 1, False))
    return x
```

```python
import functools
import math

import numpy as np
import jax
import jax.numpy as jnp
from jax import lax
from jax.experimental import pallas as pl
from jax.experimental.pallas import tpu as pltpu

F32 = jnp.float32
BF16 = jnp.bfloat16

EPS = 1e-6
LOG2E = math.log2(math.e)
LRU_C = 8.0
LRU_HEADS = 8
FNET_GROUPS = 4
RET_HEADS = 4
GRID_W = 64
ROPE_THETA = 10000.0

LANES = 128
VMEM_LIMIT = 56 * 1024 * 1024

ROW_TILE = 512
LRU_CHUNK = 256
LRU_UNROLL = 4
RET_CHUNK = 256
STATE_CHUNKS = 8
MAIN_CHUNKS = 4
FFT_N2 = 128
FFT_J = 16


def _cparams(n_axes):
    return pltpu.CompilerParams(dimension_semantics=("arbitrary",) * n_axes,
                                vmem_limit_bytes=VMEM_LIMIT)


def _silu(v):
    return v * jax.nn.sigmoid(v)


def _ada_kernel(cond_ref, w_ref, b_ref, o_ref):
    s = _silu(cond_ref[...])
    o_ref[...] = jnp.dot(s, w_ref[...], preferred_element_type=F32,
                         precision=lax.Precision.HIGHEST) + b_ref[...]


def _ada_mod(cond8, mod_w, mod_b):
    depth, d, n3 = mod_w.shape
    tn = d
    return pl.pallas_call(
        _ada_kernel,
        out_shape=jax.ShapeDtypeStruct((depth, 8, n3), F32),
        grid=(depth, n3 // tn),
        in_specs=[
            pl.BlockSpec((8, d), lambda l, j: (0, 0)),
            pl.BlockSpec((None, d, tn), lambda l, j: (l, 0, j)),
            pl.BlockSpec((None, 1, tn), lambda l, j: (l, 0, j)),
        ],
        out_specs=pl.BlockSpec((None, 8, tn), lambda l, j: (l, 0, j)),
        compiler_params=_cparams(2),
        name="ada_mod",
    )(cond8, mod_w, mod_b.reshape(depth, 1, n3))


def _proj_kernel(*refs, out_kind, has_in, rope_cols, chunk):
    it = iter(refs)
    if out_kind:
        a_ref, b_ref = next(it), next(it)
    x_ref = next(it)
    if out_kind:
        wo_ref, pg_ref, gate_ref = next(it), next(it), next(it)
    if has_in:
        g_ref, sc_ref, sh_ref, wi_ref = next(it), next(it), next(it), next(it)
        if rope_cols:
            crow_ref, srow_ref, ccol_ref, scol_ref = next(it), next(it), next(it), next(it)
    if out_kind:
        xo_ref = next(it)
    if has_in:
        u_ref = next(it)

    x = x_ref[...]
    if out_kind == "even":
        wa = a_ref.shape[-1]
        y = jnp.dot(a_ref[...], wo_ref[:wa, :], preferred_element_type=F32)
        y = y + jnp.dot(b_ref[...], wo_ref[wa:, :], preferred_element_type=F32)
    elif out_kind == "odd":
        m = (a_ref[...].astype(F32) * _silu(b_ref[...].astype(F32))).astype(BF16)
        y = jnp.dot(m, wo_ref[...], preferred_element_type=F32)
    if out_kind:
        ms = jnp.mean(y * y, axis=-1, keepdims=True)
        x = x + gate_ref[...] * ((y * lax.rsqrt(ms + EPS)) * pg_ref[...])
        xo_ref[...] = x
    if not has_in:
        return

    ms = jnp.mean(x * x, axis=-1, keepdims=True)
    h = (x * lax.rsqrt(ms + EPS)) * g_ref[...]
    h = h * (1.0 + sc_ref[...]) + sh_ref[...]
    hb = h.astype(BF16)
    if rope_cols:
        nrow = crow_ref.shape[0]
        expand = lambda r: jnp.concatenate(
            [jnp.broadcast_to(r[g:g + 1, :], (GRID_W, LANES)) for g in range(nrow)], axis=0)
        repeat = lambda r: jnp.concatenate([r[...]] * nrow, axis=0)
        cos_t = (expand(crow_ref), repeat(ccol_ref))
        sin_t = (expand(srow_ref), repeat(scol_ref))
    n_out = u_ref.shape[-1]
    for c0 in range(0, n_out, chunk):
        r = jnp.dot(hb, wi_ref[:, c0:c0 + chunk], preferred_element_type=F32)
        if c0 < rope_cols:
            parts = []
            for s0 in range(0, chunk, LANES):
                axis = ((c0 + s0) // LANES) % 2
                ra = r[:, s0:s0 + LANES]
                parts.append(ra * cos_t[axis] + pltpu.roll(ra, LANES // 2, axis=1) * sin_t[axis])
            r = jnp.concatenate(parts, axis=1)
        u_ref[:, c0:c0 + chunk] = r.astype(u_ref.dtype)


def _proj(x, out=None, inp=None):
    b, t, d = x.shape
    tm = min(ROW_TILE, t)
    row = lambda bi, i: (bi, i, 0)
    const2 = lambda bi, i: (0, 0)
    per_b = lambda bi, i: (bi, 0, 0)
    in_specs, args, out_shape, out_specs = [], [], [], []
    out_kind = None
    if out is not None:
        out_kind, a, bb, w_out, pg, gate = out
        ka = a.shape[-1]
        in_specs.append(pl.BlockSpec((None, tm, ka), row))
        if out_kind == "even":
            in_specs.append(pl.BlockSpec((None, tm, bb.shape[-1]), row))
        else:
            g_blk = bb.shape[-1] // ka - 1
            in_specs.append(pl.BlockSpec((None, tm, ka), lambda bi, i: (bi, i, g_blk)))
        args += [a, bb]
    in_specs.append(pl.BlockSpec((None, tm, d), row))
    args.append(x)
    if out is not None:
        in_specs += [pl.BlockSpec(w_out.shape, const2, pipeline_mode=pl.Buffered(1)),
                     pl.BlockSpec((1, d), const2), pl.BlockSpec((None, 1, d), per_b)]
        args += [w_out, pg, gate]
        out_shape.append(jax.ShapeDtypeStruct((b, t, d), F32))
        out_specs.append(pl.BlockSpec((None, tm, d), row))
    rope_cols = 0
    if inp is not None:
        g, scale, shift, w_in, rope = inp
        n = w_in.shape[1]
        in_specs += [pl.BlockSpec((1, d), const2), pl.BlockSpec((None, 1, d), per_b),
                     pl.BlockSpec((None, 1, d), per_b),
                     pl.BlockSpec((d, n), const2, pipeline_mode=pl.Buffered(1))]
        args += [g, scale, shift, w_in]
        if rope is not None:
            crow, srow, ccol, scol, rope_cols = rope
            assert tm % GRID_W == 0
            nrow = tm // GRID_W
            in_specs += [pl.BlockSpec((nrow, LANES), lambda bi, i: (i, 0)),
                         pl.BlockSpec((nrow, LANES), lambda bi, i: (i, 0)),
                         pl.BlockSpec((GRID_W, LANES), const2), pl.BlockSpec((GRID_W, LANES), const2)]
            args += [crow, srow, ccol, scol]
        out_shape.append(jax.ShapeDtypeStruct((b, t, n), BF16))
        out_specs.append(pl.BlockSpec((None, tm, n), row))
    res = pl.pallas_call(
        functools.partial(_proj_kernel, out_kind=out_kind, has_in=inp is not None,
                          rope_cols=rope_cols, chunk=2 * LANES),
        out_shape=tuple(out_shape),
        grid=(b, t // tm),
        in_specs=in_specs,
        out_specs=tuple(out_specs),
        compiler_params=_cparams(2),
        name="proj_" + (out_kind or "first") + ("_in" if inp is not None else "_last"),
    )(*args)
    if out is None:
        return None, res[0]
    if inp is None:
        return res[0], None
    return res


def _group_scan(a, b, reverse):
    rows, lanes = a.shape
    ng = rows // 8
    a = a.reshape(ng, 8, lanes)
    b = b.reshape(ng, 8, lanes)
    sub = lax.broadcasted_iota(jnp.int32, (1, 8, lanes), 1)
    for dd in (1, 2, 4):
        if reverse:
            m = sub <= 7 - dd
            sh = 8 - dd
        else:
            m = sub >= dd
            sh = dd
        a_sh = pltpu.roll(a, sh, axis=1)
        b_sh = pltpu.roll(b, sh, axis=1)
        b = b + a * jnp.where(m, b_sh, 0.0)
        a = a * jnp.where(m, a_sh, 1.0)
    return a.reshape(rows, lanes), b.reshape(rows, lanes)


def _carry_chain(a, b, carry, reverse):
    ng = a.shape[0] // 8
    outs = [None] * ng
    order = range(ng - 1, -1, -1) if reverse else range(ng)
    for gi in order:
        hg = b[8 * gi:8 * gi + 8] + a[8 * gi:8 * gi + 8] * carry
        carry = hg[0:1] if reverse else hg[7:8]
        outs[gi] = hg
    return jnp.concatenate(outs, axis=0), carry


def _scan_chunk(a, b, carry, reverse):
    a, b = _group_scan(a, b, reverse)
    return _carry_chain(a, b, carry, reverse)


def _sqrt_nonneg(v):
    return jnp.where(v > 0.0, v * lax.rsqrt(v), 0.0)


def _lru_kernel(xa_ref, ga_ref, xca_ref, gca_ref, cw_ref, cb_ref, wg_ref, bg_ref, lam_ref,
                o_ref, oc_ref, zs_ref, hb_ref):
    cw = cw_ref[...]
    cb = cb_ref[...]
    nsp2 = []
    for d in (0, 1):
        y = -lam_ref[d]
        nsp2.append((-LRU_C * LOG2E) * (jnp.maximum(y, 0.0) + jnp.log1p(jnp.exp(-jnp.abs(y)))))

    def run_seq(x_ref, g_ref, out_ref, carry_f, carry_r):
        tn = x_ref.shape[0]
        rn = min(LRU_CHUNK, tn)
        nch = tn // rn
        assert nch == 1 or nch % 2 == 0

        def conv_body(c, carry):
            r0 = pl.multiple_of(c * rn, rn)
            p0 = pl.multiple_of(jnp.maximum(r0 - 16, 0), 16)
            n0 = pl.multiple_of(jnp.minimum(r0 + rn, tn - 16), 16)
            cur = x_ref[pl.ds(r0, rn), :].astype(F32)
            prev = jnp.where(c > 0, x_ref[pl.ds(p0, 16), :].astype(F32), 0.0)
            nxt = jnp.where(c < nch - 1, x_ref[pl.ds(n0, 16), :].astype(F32), 0.0)
            ext = jnp.concatenate([prev, cur, nxt], axis=0)
            z = cb + cw[0:1] * pltpu.roll(ext, 2, axis=0)[16:16 + rn]
            z = z + cw[1:2] * pltpu.roll(ext, 1, axis=0)[16:16 + rn]
            z = z + cw[2:3] * cur
            z = z + cw[3:4] * pltpu.roll(ext, rn + 31, axis=0)[16:16 + rn]
            zs_ref[pl.ds(r0, rn), :] = z
            return carry

        lax.fori_loop(0, nch, conv_body, 0)

        def coeffs(r0, d):
            z = zs_ref[pl.ds(r0, rn), :]
            e = jnp.exp2(jnp.dot(z.astype(BF16), wg_ref[d], preferred_element_type=F32) + bg_ref[d])
            r = 1.0 / (1.0 + e[:, :LANES])
            i = 1.0 / (1.0 + e[:, LANES:])
            a = jnp.exp2(nsp2[d] * r)
            return a, _sqrt_nonneg(1.0 - a * a) * (i * z)

        def gate(r0):
            return _silu(g_ref[pl.ds(r0, rn), :].astype(F32))

        if nch == 1:
            a, b = coeffs(0, 0)
            hf, carry_f = _scan_chunk(a, b, carry_f, False)
            a, b = coeffs(0, 1)
            hr, carry_r = _scan_chunk(a, b, carry_r, True)
            out_ref[...] = ((hf + hr) * gate(0)).astype(out_ref.dtype)
            return carry_f, carry_r

        def step(c, carry, combine):
            cf, cr = carry
            r0 = pl.multiple_of(c * rn, rn)
            r1 = pl.multiple_of((nch - 1 - c) * rn, rn)
            a, b = coeffs(r0, 0)
            hf, cf = _scan_chunk(a, b, cf, False)
            a, b = coeffs(r1, 1)
            hr, cr = _scan_chunk(a, b, cr, True)
            if combine:
                out_ref[pl.ds(r0, rn), :] = ((hf + hb_ref[pl.ds(r0, rn), :]) * gate(r0)).astype(out_ref.dtype)
                out_ref[pl.ds(r1, rn), :] = ((hb_ref[pl.ds(r1, rn), :] + hr) * gate(r1)).astype(out_ref.dtype)
            else:
                hb_ref[pl.ds(r0, rn), :] = hf
                hb_ref[pl.ds(r1, rn), :] = hr
            return cf, cr

        carry = lax.fori_loop(0, nch // 2, functools.partial(step, combine=False), (carry_f, carry_r),
                              unroll=LRU_UNROLL)
        return lax.fori_loop(nch // 2, nch, functools.partial(step, combine=True), carry,
                             unroll=LRU_UNROLL)

    zero = jnp.zeros((1, LANES), F32)
    cf, cr = run_seq(xca_ref, gca_ref, oc_ref, zero, zero)
    run_seq(xa_ref, ga_ref, o_ref, cf, cr)


def _lru_branch(u, uc, conv_w, conv_b, wg, bg, lam):
    b, t, _ = u.shape
    tc = uc.shape[1]
    w = conv_w.shape[1]
    nh = w // LANES
    ga0 = 2 * nh
    return pl.pallas_call(
        _lru_kernel,
        out_shape=(jax.ShapeDtypeStruct((b, t, w), BF16), jax.ShapeDtypeStruct((b, tc, w), BF16)),
        grid=(b, nh),
        in_specs=[
            pl.BlockSpec((None, t, LANES), lambda bi, h: (bi, 0, h)),
            pl.BlockSpec((None, t, LANES), lambda bi, h: (bi, 0, ga0 + h)),
            pl.BlockSpec((None, tc, LANES), lambda bi, h: (bi, 0, h)),
            pl.BlockSpec((None, tc, LANES), lambda bi, h: (bi, 0, ga0 + h)),
            pl.BlockSpec((conv_w.shape[0], LANES), lambda bi, h: (0, h)),
            pl.BlockSpec((1, LANES), lambda bi, h: (0, h)),
            pl.BlockSpec((2, None, LANES, 2 * LANES), lambda bi, h: (0, h, 0, 0)),
            pl.BlockSpec((2, None, 1, 2 * LANES), lambda bi, h: (0, h, 0, 0)),
            pl.BlockSpec((2, None, 1, LANES), lambda bi, h: (0, h, 0, 0)),
        ],
        out_specs=(pl.BlockSpec((None, t, LANES), lambda bi, h: (bi, 0, h)),
                   pl.BlockSpec((None, tc, LANES), lambda bi, h: (bi, 0, h))),
        scratch_shapes=[pltpu.VMEM((t, LANES), F32), pltpu.VMEM((t, LANES), F32)],
        compiler_params=_cparams(2),
        name="rglru",
    )(u, u, uc, uc, conv_w, conv_b, wg, bg, lam)


def _w12_kernel(cs_ref, wf_ref, o_ref):
    gd = wf_ref.shape[0]
    for half in range(gd // LANES):
        wf = wf_ref[:, half * LANES:(half + 1) * LANES]
        o_ref[half, :, :LANES] = jnp.dot(cs_ref[:gd, :], wf, preferred_element_type=F32,
                                         precision=lax.Precision.HIGHEST).astype(o_ref.dtype)
        o_ref[half, :, LANES:] = jnp.dot(cs_ref[gd:, :], wf, preferred_element_type=F32,
                                         precision=lax.Precision.HIGHEST).astype(o_ref.dtype)


def _fourier_weights(fnet_w):
    ne, ng, gd, _ = fnet_w.shape
    nhalf = gd // LANES
    idx = np.arange(gd)
    ang = 2.0 * np.pi * ((idx[:, None] * idx[None, :]) % gd) / gd
    cs = np.concatenate([np.cos(ang), np.sin(ang)], axis=0) / math.sqrt(gd)
    out = pl.pallas_call(
        _w12_kernel,
        out_shape=jax.ShapeDtypeStruct((ne, ng, nhalf, gd, 2 * LANES), BF16),
        grid=(ne, ng),
        in_specs=[pl.BlockSpec((2 * gd, gd), lambda e, g: (0, 0)),
                  pl.BlockSpec((None, None, gd, gd), lambda e, g: (e, g, 0, 0))],
        out_specs=pl.BlockSpec((None, None, nhalf, gd, 2 * LANES), lambda e, g: (e, g, 0, 0, 0)),
        compiler_params=_cparams(2),
        name="fourier_weights",
    )(jnp.asarray(cs, F32), fnet_w)
    return out.reshape(ne, ng * nhalf, gd, 2 * LANES)


def _chanmix_kernel(xb_ref, w_ref, ur_ref, ui_ref):
    ncb, gd, _ = w_ref.shape
    per_group = gd // LANES
    for cb in range(ncb):
        g = cb // per_group
        r = jnp.dot(xb_ref[:, g * gd:(g + 1) * gd], w_ref[cb], preferred_element_type=F32)
        ur_ref[:, cb * LANES:(cb + 1) * LANES] = r[:, :LANES]
        ui_ref[:, cb * LANES:(cb + 1) * LANES] = r[:, LANES:]


def _chanmix(u, w12):
    b, t, _ = u.shape
    ncb, gd, _ = w12.shape
    w = ncb * LANES
    tm = min(ROW_TILE, t)
    out = jax.ShapeDtypeStruct((b, t, w), F32)
    return pl.pallas_call(
        _chanmix_kernel,
        out_shape=(out, out),
        grid=(b, t // tm),
        in_specs=[pl.BlockSpec((None, tm, w), lambda bi, i: (bi, i, 1)),
                  pl.BlockSpec((ncb, gd, 2 * LANES), lambda bi, i: (0, 0, 0))],
        out_specs=(pl.BlockSpec((None, tm, w), lambda bi, i: (bi, i, 0)),
                   pl.BlockSpec((None, tm, w), lambda bi, i: (bi, i, 0))),
        compiler_params=_cparams(2),
        name="fourier_chanmix",
    )(u, w12)


def _dft_a_kernel(xb_ref, w_ref, ea_ref, twr_ref, twi_ref, br_ref, bi_ref, u_s):
    n2, nj, gd = xb_ref.shape
    ncol = w_ref.shape[0]
    xb = xb_ref[...].reshape(n2 * nj, gd)
    for c in range(ncol):
        u = jnp.dot(xb, w_ref[c], preferred_element_type=F32)
        u_s[2 * c] = u[:, :LANES]
        u_s[2 * c + 1] = u[:, LANES:]
    for c in range(ncol):
        ur_s, ui_s = u_s.at[2 * c], u_s.at[2 * c + 1]
        xr = jnp.concatenate([ur_s[pl.ds(j, n2, stride=nj), :] for j in range(nj)], axis=1)
        xi = jnp.concatenate([ui_s[pl.ds(j, n2, stride=nj), :] for j in range(nj)], axis=1)
        rhs = jnp.concatenate([xr, xi], axis=0).astype(BF16)
        res = jnp.dot(ea_ref[...], rhs, preferred_element_type=F32)
        cols = slice(c * LANES, (c + 1) * LANES)
        for j in range(nj):
            pr = res[:n2, j * LANES:(j + 1) * LANES]
            pi = res[n2:, j * LANES:(j + 1) * LANES]
            tr = twr_ref[j]
            ti = twi_ref[j]
            br_ref[j, :, cols] = (pr * tr - pi * ti).astype(br_ref.dtype)
            bi_ref[j, :, cols] = (pr * ti + pi * tr).astype(bi_ref.dtype)


def _dft_b_kernel(br_ref, bi_ref, gb_ref, eb_ref, o_ref, s_ref):
    n1, nj, width = br_ref.shape
    ncol = width // LANES
    for c in range(ncol):
        cols = slice(c * LANES, (c + 1) * LANES)
        sr_ref, si_ref, so_ref = s_ref.at[3 * c], s_ref.at[3 * c + 1], s_ref.at[3 * c + 2]
        sr_ref[...] = br_ref[:, :, cols].astype(F32).reshape(n1 * nj, LANES)
        si_ref[...] = bi_ref[:, :, cols].astype(F32).reshape(n1 * nj, LANES)
        xr = jnp.concatenate([sr_ref[pl.ds(j, n1, stride=nj), :] for j in range(nj)], axis=1)
        xi = jnp.concatenate([si_ref[pl.ds(j, n1, stride=nj), :] for j in range(nj)], axis=1)
        rhs = jnp.concatenate([xr, xi], axis=0).astype(BF16)
        res = jnp.dot(eb_ref[...], rhs, preferred_element_type=F32)
        for j in range(nj):
            so_ref[pl.ds(j, n1, stride=nj), :] = res[:, j * LANES:(j + 1) * LANES]
        fy = so_ref[...].reshape(n1, nj, LANES)
        o_ref[:, :, cols] = (fy * _silu(gb_ref[:, :, cols].astype(F32))).astype(o_ref.dtype)


def _dft_consts(t):
    n2 = FFT_N2
    n1 = t // n2
    i2 = np.arange(n2)
    a2 = 2.0 * np.pi * ((i2[:, None] * i2[None, :]) % n2) / n2
    er, ei = np.cos(a2), np.sin(a2)
    ea = np.block([[er, -ei], [ei, er]])
    i1 = np.arange(n1)
    atw = 2.0 * np.pi * ((i1[:, None] * i2[None, :]) % t) / t
    a1 = 2.0 * np.pi * ((i1[:, None] * i1[None, :]) % n1) / n1
    eb = np.concatenate([np.cos(a1), -np.sin(a1)], axis=1) / math.sqrt(t)
    twr = jnp.broadcast_to(jnp.asarray(np.cos(atw), F32)[:, :, None], (n1, n2, LANES))
    twi = jnp.broadcast_to(jnp.asarray(np.sin(atw), F32)[:, :, None], (n1, n2, LANES))
    return jnp.asarray(ea, F32).astype(BF16), twr, twi, jnp.asarray(eb, F32).astype(BF16)


def _seq_fourier(u, w12, consts):
    b, t, n_in = u.shape
    ncb, gd, _ = w12.shape
    w = ncb * LANES
    n2 = FFT_N2
    n1 = t // n2
    nj = FFT_J
    ea, twr, twi, eb = consts
    u4 = u.reshape(b, n2, n1, n_in)
    xb0 = w // gd
    per_group = gd // LANES
    mid = jax.ShapeDtypeStruct((b, n1, n2, w), BF16)
    br, bi = pl.pallas_call(
        _dft_a_kernel,
        out_shape=(mid, mid),
        grid=(n1 // nj, b, ncb // per_group),
        in_specs=[
            pl.BlockSpec((None, n2, nj, gd), lambda tb, bi_, g: (bi_, 0, tb, xb0 + g)),
            pl.BlockSpec((per_group, gd, 2 * LANES), lambda tb, bi_, g: (g, 0, 0)),
            pl.BlockSpec((2 * n2, 2 * n2), lambda tb, bi_, g: (0, 0)),
            pl.BlockSpec((nj, n2, LANES), lambda tb, bi_, g: (tb, 0, 0)),
            pl.BlockSpec((nj, n2, LANES), lambda tb, bi_, g: (tb, 0, 0)),
        ],
        out_specs=(pl.BlockSpec((None, nj, n2, gd), lambda tb, bi_, g: (bi_, tb, 0, g)),
                   pl.BlockSpec((None, nj, n2, gd), lambda tb, bi_, g: (bi_, tb, 0, g))),
        scratch_shapes=[pltpu.VMEM((2 * per_group, n2 * nj, LANES), F32)],
        compiler_params=_cparams(3),
        name="dft_stage_a",
    )(u4, w12, ea, twr, twi)
    gb0 = (n_in - w) // gd
    out = pl.pallas_call(
        _dft_b_kernel,
        out_shape=jax.ShapeDtypeStruct((b, n1, n2, w), BF16),
        grid=(b, n2 // nj, w // gd),
        in_specs=[
            pl.BlockSpec((None, n1, nj, gd), lambda bi_, kb, g: (bi_, 0, kb, g)),
            pl.BlockSpec((None, n1, nj, gd), lambda bi_, kb, g: (bi_, 0, kb, g)),
            pl.BlockSpec((None, n1, nj, gd), lambda bi_, kb, g: (bi_, 0, kb, gb0 + g)),
            pl.BlockSpec((n1, 2 * n1), lambda bi_, kb, g: (0, 0)),
        ],
        out_specs=pl.BlockSpec((None, n1, nj, gd), lambda bi_, kb, g: (bi_, 0, kb, g)),
        scratch_shapes=[pltpu.VMEM((3 * per_group, n1 * nj, LANES), F32)],
        compiler_params=_cparams(3),
        name="dft_stage_b",
    )(br, bi, u.reshape(b, n1, n2, n_in), eb)
    return out.reshape(b, t, w)


def _dft_dense_kernel(ur_ref, ui_ref, gb_ref, ec_ref, o_ref):
    rhs = jnp.concatenate([ur_ref[...], ui_ref[...]], axis=0).astype(BF16)
    fy = jnp.dot(ec_ref[...], rhs, preferred_element_type=F32)
    o_ref[...] = (fy * _silu(gb_ref[...].astype(F32))).astype(o_ref.dtype)


def _seq_dft_dense(ur, ui, u):
    b, t, w = ur.shape
    gb_blk = u.shape[-1] // w - 1
    it = np.arange(t)
    ang = 2.0 * np.pi * ((it[:, None] * it[None, :]) % t) / t
    ec = jnp.asarray(np.concatenate([np.cos(ang), -np.sin(ang)], axis=1) / math.sqrt(t), F32).astype(BF16)
    return pl.pallas_call(
        _dft_dense_kernel,
        out_shape=jax.ShapeDtypeStruct((b, t, w), BF16),
        grid=(b,),
        in_specs=[pl.BlockSpec((None, t, w), lambda bi: (bi, 0, 0)),
                  pl.BlockSpec((None, t, w), lambda bi: (bi, 0, 0)),
                  pl.BlockSpec((None, t, w), lambda bi: (bi, 0, gb_blk)),
                  pl.BlockSpec((t, 2 * t), lambda bi: (0, 0))],
        out_specs=pl.BlockSpec((None, t, w), lambda bi: (bi, 0, 0)),
        compiler_params=_cparams(1),
        name="dft_dense",
    )(ur, ui, u, ec)


def _ret_state_kernel(lg_ref, k_ref, v_ref, s0_ref, sr_ref, se_ref, s_ref, kd_ref):
    nh, dk, dv = s_ref.shape
    nsub = sr_ref.shape[0]
    cr = k_ref.shape[0] // nsub

    @pl.when(pl.program_id(1) == 0)
    def _init():
        s_ref[...] = s0_ref[...]
        rk = lax.broadcasted_iota(jnp.int32, (cr, dk), 0).astype(F32)
        for h in range(nh):
            kd_ref[h] = jnp.exp(lg_ref[1, h] * rk)

    for sub in range(nsub - 1, -1, -1):
        rows = slice(sub * cr, (sub + 1) * cr)
        for h in range(nh):
            cdec = jnp.exp(jnp.full((1, dv), lg_ref[1, h] * cr, F32))
            st = s_ref[h]
            sr_ref[sub, h] = st.astype(sr_ref.dtype)
            kdec = (k_ref[rows, h * dk:(h + 1) * dk].astype(F32) * kd_ref[h]).astype(BF16)
            s_ref[h] = cdec * st + lax.dot_general(kdec, v_ref[rows, h * dv:(h + 1) * dv],
                                                   (((0,), (0,)), ((), ())), preferred_element_type=F32)

    @pl.when(pl.program_id(1) == pl.num_programs(1) - 1)
    def _fin():
        se_ref[...] = s_ref[...]


def _ret_main_kernel(lg_ref, q_ref, k_ref, v_ref, sr_ref, s0_ref, o_ref, se_ref,
                     s_ref, dm_ref, qf_ref, qr_ref, kd_ref):
    nh, dk, dv = s_ref.shape
    nsub = sr_ref.shape[0]
    cr = q_ref.shape[0] // nsub

    @pl.when(pl.program_id(1) == 0)
    def _init():
        s_ref[...] = s0_ref[...]
        ii = lax.broadcasted_iota(jnp.int32, (cr, cr), 0).astype(F32)
        jj = lax.broadcasted_iota(jnp.int32, (cr, cr), 1).astype(F32)
        diff = jj - ii
        past = diff <= 0
        rk = lax.broadcasted_iota(jnp.int32, (cr, dk), 0).astype(F32)
        for h in range(nh):
            lgf = lg_ref[0, h]
            lgr = lg_ref[1, h]
            dm_ref[h] = jnp.where(past, jnp.exp(lgf * jnp.where(past, -diff, 0.0)),
                                  jnp.exp(lgr * jnp.where(past, 0.0, diff)))
            qf_ref[h] = jnp.exp(lgf * (rk + 1.0))
            qr_ref[h] = jnp.exp(lgr * (cr - rk))
            kd_ref[h] = jnp.exp(lgf * ((cr - 1.0) - rk))

    for sub in range(nsub):
        rows = slice(sub * cr, (sub + 1) * cr)
        for h in range(nh):
            cdec = jnp.exp(jnp.full((1, dv), lg_ref[0, h] * cr, F32))
            qh = q_ref[rows, h * dk:(h + 1) * dk]
            kh = k_ref[rows, h * dk:(h + 1) * dk]
            vh = v_ref[rows, h * dv:(h + 1) * dv]
            s = lax.dot_general(qh, kh, (((1,), (1,)), ((), ())), preferred_element_type=F32)
            st = s_ref[h]
            qf = qh.astype(F32)
            lhs = jnp.concatenate([(s * dm_ref[h]).astype(BF16), (qf * qf_ref[h]).astype(BF16),
                                   (qf * qr_ref[h]).astype(BF16)], axis=1)
            rhs = jnp.concatenate([vh, st.astype(BF16), sr_ref[sub, h]], axis=0)
            o = jnp.dot(lhs, rhs, preferred_element_type=F32)
            kdec = (kh.astype(F32) * kd_ref[h]).astype(BF16)
            s_ref[h] = cdec * st + lax.dot_general(kdec, vh, (((0,), (0,)), ((), ())),
                                                   preferred_element_type=F32)
            mu = jnp.mean(o, axis=-1, keepdims=True)
            oc = o - mu
            var = jnp.mean(oc * oc, axis=-1, keepdims=True)
            o_ref[rows, h * dv:(h + 1) * dv] = (oc * lax.rsqrt(var + EPS)).astype(o_ref.dtype)

    @pl.when(pl.program_id(1) == pl.num_programs(1) - 1)
    def _fin():
        se_ref[...] = s_ref[...]


def _ret_seq(u, lg, s0_f, s0_r):
    b, t, _ = u.shape
    _, nh, dk, dv = s0_f.shape
    cr = min(RET_CHUNK, t)
    nc = t // cr
    qk = nh * dk
    vw = nh * dv
    v_blk = (2 * qk) // vw
    state = jax.ShapeDtypeStruct((b, nh, dk, dv), F32)
    state_spec = pl.BlockSpec((None, nh, dk, dv), lambda bi, i: (bi, 0, 0, 0))
    smem = pl.BlockSpec(memory_space=pltpu.SMEM)
    nsub = math.gcd(nc, STATE_CHUNKS)
    nmain = math.gcd(nc, MAIN_CHUNKS)
    ns = nc // nsub
    sr, se_r = pl.pallas_call(
        _ret_state_kernel,
        out_shape=(jax.ShapeDtypeStruct((b, nc, nh, dk, dv), BF16), state),
        grid=(b, ns),
        in_specs=[smem,
                  pl.BlockSpec((None, nsub * cr, qk), lambda bi, i: (bi, ns - 1 - i, 1)),
                  pl.BlockSpec((None, nsub * cr, vw), lambda bi, i: (bi, ns - 1 - i, v_blk)),
                  state_spec],
        out_specs=(pl.BlockSpec((None, nsub, nh, dk, dv), lambda bi, i: (bi, ns - 1 - i, 0, 0, 0)),
                   state_spec),
        scratch_shapes=[pltpu.VMEM((nh, dk, dv), F32), pltpu.VMEM((nh, cr, dk), F32)],
        compiler_params=_cparams(2),
        name="retention_state",
    )(lg, u, u, s0_r)
    o, se_f = pl.pallas_call(
        _ret_main_kernel,
        out_shape=(jax.ShapeDtypeStruct((b, t, vw), BF16), state),
        grid=(b, nc // nmain),
        in_specs=[smem,
                  pl.BlockSpec((None, nmain * cr, qk), lambda bi, i: (bi, i, 0)),
                  pl.BlockSpec((None, nmain * cr, qk), lambda bi, i: (bi, i, 1)),
                  pl.BlockSpec((None, nmain * cr, vw), lambda bi, i: (bi, i, v_blk)),
                  pl.BlockSpec((None, nmain, nh, dk, dv), lambda bi, i: (bi, i, 0, 0, 0)),
                  state_spec],
        out_specs=(pl.BlockSpec((None, nmain * cr, vw), lambda bi, i: (bi, i, 0)), state_spec),
        scratch_shapes=[
            pltpu.VMEM((nh, dk, dv), F32),
            pltpu.VMEM((nh, cr, cr), F32),
            pltpu.VMEM((nh, cr, dk), F32),
            pltpu.VMEM((nh, cr, dk), F32),
            pltpu.VMEM((nh, cr, dk), F32),
        ],
        compiler_params=_cparams(2),
        name="retention_main",
    )(lg, u, u, u, sr, s0_f)
    return o, se_f, se_r


def _retention(u, uc, lg):
    b = u.shape[0]
    nh = RET_HEADS
    dk = u.shape[-1] // (6 * nh)
    zero = jnp.zeros((b, nh, dk, 2 * dk), F32)
    mc, s_f, s_r = _ret_seq(uc, lg, zero, zero)
    m, _, _ = _ret_seq(u, lg, s_f, s_r)
    return m, mc


def _rope_tables(t):
    nf = LANES // 2
    inv = ROPE_THETA ** (-jnp.arange(nf, dtype=F32) / nf)
    ar = jnp.arange(t // GRID_W, dtype=F32)[:, None] * inv
    ac = jnp.arange(GRID_W, dtype=F32)[:, None] * inv
    dup = lambda v: jnp.concatenate([v, v], axis=1)
    sgn = lambda v: jnp.concatenate([-v, v], axis=1)
    return dup(jnp.cos(ar)), sgn(jnp.sin(ar)), dup(jnp.cos(ac)), sgn(jnp.sin(ac))


def kernel(x, c, ctx, c_ctx, mod_w, mod_b, pre_g, post_g, mix_w_in, mix_w_out, conv_w, conv_b,
           lru_wa, lru_ba, lru_wx, lru_bx, lru_lam, fnet_w, ret_w_in, ret_w_out, ret_log_gamma):
    b, t, d = x.shape
    tc = ctx.shape[1]
    depth = mod_w.shape[0]
    assert t % ROW_TILE == 0 and t % (FFT_N2 * FFT_J) == 0 and (t // FFT_N2) % FFT_J == 0
    assert tc % GRID_W == 0 and tc % 16 == 0 and tc <= ROW_TILE and b + 1 <= 8

    cond8 = jnp.zeros((8, d), F32).at[:b].set(c).at[b].set(c_ctx)
    mods = _ada_mod(cond8, mod_w, mod_b)

    w12 = _fourier_weights(fnet_w)
    dft_consts = _dft_consts(t)
    qk = ret_w_in.shape[-1] // 6
    rope_l = _rope_tables(t) + (2 * qk,)
    ones = lambda n: jnp.ones((n, LANES), F32)
    zeros = lambda n: jnp.zeros((n, LANES), F32)
    rope_c = (ones(tc // GRID_W), zeros(tc // GRID_W), ones(GRID_W), zeros(GRID_W), 2 * qk)

    nh_l = LRU_HEADS
    hd = lru_wa.shape[-1]
    wg_all = (jnp.concatenate([lru_wa, lru_wx], axis=-1) * (-LOG2E)).astype(BF16)
    bg_all = jnp.concatenate([lru_ba.reshape(-1, 2, nh_l, 1, hd),
                              lru_bx.reshape(-1, 2, nh_l, 1, hd)], axis=-1) * (-LOG2E)
    lam_all = lru_lam.reshape(-1, 2, nh_l, 1, hd)

    kscale = jnp.concatenate([jnp.ones((qk,), F32), jnp.full((qk,), (qk // RET_HEADS) ** -0.5, F32),
                              jnp.ones((4 * qk,), F32)])

    def w_in_of(layer):
        if layer % 2 == 0:
            return mix_w_in[layer // 2].astype(BF16)
        return (ret_w_in[layer // 2] * kscale).astype(BF16)

    def mod_of(layer, latent):
        m_l = mods[layer]
        parts = (m_l[:, :d], m_l[:, d:2 * d], m_l[:, 2 * d:])
        if latent:
            return tuple(v[:b].reshape(b, 1, d) for v in parts)
        return tuple(jnp.broadcast_to(v[b].reshape(1, 1, d), (b, 1, d)) for v in parts)

    def inp_of(layer, latent):
        shift, scale, _ = mod_of(layer, latent)
        rope = None if layer % 2 == 0 else (rope_l if latent else rope_c)
        return (pre_g[layer].reshape(1, d), scale, shift, w_in_of(layer), rope)

    xc = ctx
    _, u = _proj(x, inp=inp_of(0, True))
    _, uc = _proj(xc, inp=inp_of(0, False))
    for layer in range(depth):
        last = layer == depth - 1
        post = post_g[layer].reshape(1, d)
        gt_l = mod_of(layer, True)[2]
        gt_c = mod_of(layer, False)[2]
        if layer % 2 == 0:
            e = layer // 2
            w_out = mix_w_out[e].astype(BF16)
            ma, mca = _lru_branch(u, uc, conv_w[e], conv_b[e].reshape(1, -1),
                                  wg_all[e], bg_all[e], lam_all[e])
            mb = _seq_fourier(u, w12[e], dft_consts)
            out_l = ("even", ma, mb, w_out, post, gt_l)
            if not last:
                urc, uic = _chanmix(uc, w12[e])
                out_c = ("even", mca, _seq_dft_dense(urc, uic, uc), w_out, post, gt_c)
        else:
            j = layer // 2
            w_out = ret_w_out[j].astype(BF16)
            m, mc = _retention(u, uc, ret_log_gamma[j])
            out_l = ("odd", m, u, w_out, post, gt_l)
            out_c = ("odd", mc, uc, w_out, post, gt_c)
        if last:
            x, _ = _proj(x, out=out_l)
        else:
            x, u = _proj(x, out=out_l, inp=inp_of(layer + 1, True))
            xc, uc = _proj(xc, out=out_c, inp=inp_of(layer + 1, False))
    return x
```

```python
import functools
import math

import numpy as np
import jax
import jax.numpy as jnp
from jax import lax
from jax.experimental import pallas as pl
from jax.experimental.pallas import tpu as pltpu

F32 = jnp.float32
BF16 = jnp.bfloat16

EPS = 1e-6
LOG2E = math.log2(math.e)
LRU_C = 8.0
LRU_HEADS = 8
FNET_GROUPS = 4
RET_HEADS = 4
GRID_W = 64
ROPE_THETA = 10000.0

LANES = 128
VMEM_LIMIT = 56 * 1024 * 1024

ROW_TILE = 512
LRU_CHUNK = 256
LRU_UNROLL = 8
RET_CHUNK = 256
STATE_CHUNKS = 8
MAIN_CHUNKS = 4
FFT_N2 = 128
FFT_J = 16


def _cparams(n_axes):
    return pltpu.CompilerParams(dimension_semantics=("arbitrary",) * n_axes,
                                vmem_limit_bytes=VMEM_LIMIT)


def _silu(v):
    return v * jax.nn.sigmoid(v)


def _ada_kernel(cond_ref, w_ref, b_ref, o_ref):
    s = _silu(cond_ref[...])
    o_ref[...] = jnp.dot(s, w_ref[...], preferred_element_type=F32,
                         precision=lax.Precision.HIGHEST) + b_ref[...]


def _ada_mod(cond8, mod_w, mod_b):
    depth, d, n3 = mod_w.shape
    tn = d
    return pl.pallas_call(
        _ada_kernel,
        out_shape=jax.ShapeDtypeStruct((depth, 8, n3), F32),
        grid=(depth, n3 // tn),
        in_specs=[
            pl.BlockSpec((8, d), lambda l, j: (0, 0)),
            pl.BlockSpec((None, d, tn), lambda l, j: (l, 0, j)),
            pl.BlockSpec((None, 1, tn), lambda l, j: (l, 0, j)),
        ],
        out_specs=pl.BlockSpec((None, 8, tn), lambda l, j: (l, 0, j)),
        compiler_params=_cparams(2),
        name="ada_mod",
    )(cond8, mod_w, mod_b.reshape(depth, 1, n3))


def _proj_kernel(*refs, out_kind, has_in, rope_cols, chunk):
    it = iter(refs)
    if out_kind:
        a_ref, b_ref = next(it), next(it)
    x_ref = next(it)
    if out_kind:
        wo_ref, pg_ref, gate_ref = next(it), next(it), next(it)
    if has_in:
        g_ref, sc_ref, sh_ref, wi_ref = next(it), next(it), next(it), next(it)
        if rope_cols:
            crow_ref, srow_ref, ccol_ref, scol_ref = next(it), next(it), next(it), next(it)
    if out_kind:
        xo_ref = next(it)
    if has_in:
        u_ref = next(it)

    x = x_ref[...]
    if out_kind == "even":
        wa = a_ref.shape[-1]
        y = jnp.dot(a_ref[...], wo_ref[:wa, :], preferred_element_type=F32)
        y = y + jnp.dot(b_ref[...], wo_ref[wa:, :], preferred_element_type=F32)
    elif out_kind == "odd":
        m = (a_ref[...].astype(F32) * _silu(b_ref[...].astype(F32))).astype(BF16)
        y = jnp.dot(m, wo_ref[...], preferred_element_type=F32)
    if out_kind:
        ms = jnp.mean(y * y, axis=-1, keepdims=True)
        x = x + gate_ref[...] * ((y * lax.rsqrt(ms + EPS)) * pg_ref[...])
        xo_ref[...] = x
    if not has_in:
        return

    ms = jnp.mean(x * x, axis=-1, keepdims=True)
    h = (x * lax.rsqrt(ms + EPS)) * g_ref[...]
    h = h * (1.0 + sc_ref[...]) + sh_ref[...]
    hb = h.astype(BF16)
    if rope_cols:
        nrow = crow_ref.shape[0]
        expand = lambda r: jnp.concatenate(
            [jnp.broadcast_to(r[g:g + 1, :], (GRID_W, LANES)) for g in range(nrow)], axis=0)
        repeat = lambda r: jnp.concatenate([r[...]] * nrow, axis=0)
        cos_t = (expand(crow_ref), repeat(ccol_ref))
        sin_t = (expand(srow_ref), repeat(scol_ref))
    n_out = u_ref.shape[-1]
    for c0 in range(0, n_out, chunk):
        r = jnp.dot(hb, wi_ref[:, c0:c0 + chunk], preferred_element_type=F32)
        if c0 < rope_cols:
            parts = []
            for s0 in range(0, chunk, LANES):
                axis = ((c0 + s0) // LANES) % 2
                ra = r[:, s0:s0 + LANES]
                parts.append(ra * cos_t[axis] + pltpu.roll(ra, LANES // 2, axis=1) * sin_t[axis])
            r = jnp.concatenate(parts, axis=1)
        u_ref[:, c0:c0 + chunk] = r.astype(u_ref.dtype)


def _proj(x, out=None, inp=None):
    b, t, d = x.shape
    tm = min(ROW_TILE, t)
    row = lambda bi, i: (bi, i, 0)
    const2 = lambda bi, i: (0, 0)
    per_b = lambda bi, i: (bi, 0, 0)
    in_specs, args, out_shape, out_specs = [], [], [], []
    out_kind = None
    if out is not None:
        out_kind, a, bb, w_out, pg, gate = out
        ka = a.shape[-1]
        in_specs.append(pl.BlockSpec((None, tm, ka), row))
        if out_kind == "even":
            in_specs.append(pl.BlockSpec((None, tm, bb.shape[-1]), row))
        else:
            g_blk = bb.shape[-1] // ka - 1
            in_specs.append(pl.BlockSpec((None, tm, ka), lambda bi, i: (bi, i, g_blk)))
        args += [a, bb]
    in_specs.append(pl.BlockSpec((None, tm, d), row))
    args.append(x)
    if out is not None:
        in_specs += [pl.BlockSpec(w_out.shape, const2, pipeline_mode=pl.Buffered(1)),
                     pl.BlockSpec((1, d), const2), pl.BlockSpec((None, 1, d), per_b)]
        args += [w_out, pg, gate]
        out_shape.append(jax.ShapeDtypeStruct((b, t, d), F32))
        out_specs.append(pl.BlockSpec((None, tm, d), row))
    rope_cols = 0
    if inp is not None:
        g, scale, shift, w_in, rope = inp
        n = w_in.shape[1]
        in_specs += [pl.BlockSpec((1, d), const2), pl.BlockSpec((None, 1, d), per_b),
                     pl.BlockSpec((None, 1, d), per_b),
                     pl.BlockSpec((d, n), const2, pipeline_mode=pl.Buffered(1))]
        args += [g, scale, shift, w_in]
        if rope is not None:
            crow, srow, ccol, scol, rope_cols = rope
            assert tm % GRID_W == 0
            nrow = tm // GRID_W
            in_specs += [pl.BlockSpec((nrow, LANES), lambda bi, i: (i, 0)),
                         pl.BlockSpec((nrow, LANES), lambda bi, i: (i, 0)),
                         pl.BlockSpec((GRID_W, LANES), const2), pl.BlockSpec((GRID_W, LANES), const2)]
            args += [crow, srow, ccol, scol]
        out_shape.append(jax.ShapeDtypeStruct((b, t, n), BF16))
        out_specs.append(pl.BlockSpec((None, tm, n), row))
    res = pl.pallas_call(
        functools.partial(_proj_kernel, out_kind=out_kind, has_in=inp is not None,
                          rope_cols=rope_cols, chunk=2 * LANES),
        out_shape=tuple(out_shape),
        grid=(b, t // tm),
        in_specs=in_specs,
        out_specs=tuple(out_specs),
        compiler_params=_cparams(2),
        name="proj_" + (out_kind or "first") + ("_in" if inp is not None else "_last"),
    )(*args)
    if out is None:
        return None, res[0]
    if inp is None:
        return res[0], None
    return res


def _group_scan(a, b, reverse):
    rows, lanes = a.shape
    ng = rows // 8
    a = a.reshape(ng, 8, lanes)
    b = b.reshape(ng, 8, lanes)
    sub = lax.broadcasted_iota(jnp.int32, (1, 8, lanes), 1)
    for dd in (1, 2, 4):
        if reverse:
            m = sub <= 7 - dd
            sh = 8 - dd
        else:
            m = sub >= dd
            sh = dd
        a_sh = pltpu.roll(a, sh, axis=1)
        b_sh = pltpu.roll(b, sh, axis=1)
        b = b + a * jnp.where(m, b_sh, 0.0)
        a = a * jnp.where(m, a_sh, 1.0)
    return a.reshape(rows, lanes), b.reshape(rows, lanes)


def _carry_chain(a, b, carry, reverse):
    ng = a.shape[0] // 8
    outs = [None] * ng
    order = range(ng - 1, -1, -1) if reverse else range(ng)
    for gi in order:
        hg = b[8 * gi:8 * gi + 8] + a[8 * gi:8 * gi + 8] * carry
        carry = hg[0:1] if reverse else hg[7:8]
        outs[gi] = hg
    return jnp.concatenate(outs, axis=0), carry


def _scan_chunk(a, b, carry, reverse):
    a, b = _group_scan(a, b, reverse)
    return _carry_chain(a, b, carry, reverse)


def _sqrt_nonneg(v):
    return jnp.where(v > 0.0, v * lax.rsqrt(v), 0.0)


def _lru_kernel(xa_ref, ga_ref, xca_ref, gca_ref, cw_ref, cb_ref, wg_ref, bg_ref, lam_ref,
                o_ref, oc_ref, zs_ref, hb_ref):
    cw = cw_ref[...]
    cb = cb_ref[...]
    nsp2 = []
    for d in (0, 1):
        y = -lam_ref[d]
        nsp2.append((-LRU_C * LOG2E) * (jnp.maximum(y, 0.0) + jnp.log1p(jnp.exp(-jnp.abs(y)))))

    def run_seq(x_ref, g_ref, out_ref, carry_f, carry_r):
        tn = x_ref.shape[0]
        rn = min(LRU_CHUNK, tn)
        nch = tn // rn
        assert nch == 1 or nch % 2 == 0

        def conv_body(c, carry):
            r0 = pl.multiple_of(c * rn, rn)
            p0 = pl.multiple_of(jnp.maximum(r0 - 16, 0), 16)
            n0 = pl.multiple_of(jnp.minimum(r0 + rn, tn - 16), 16)
            cur = x_ref[pl.ds(r0, rn), :].astype(F32)
            prev = jnp.where(c > 0, x_ref[pl.ds(p0, 16), :].astype(F32), 0.0)
            nxt = jnp.where(c < nch - 1, x_ref[pl.ds(n0, 16), :].astype(F32), 0.0)
            ext = jnp.concatenate([prev, cur, nxt], axis=0)
            z = cb + cw[0:1] * pltpu.roll(ext, 2, axis=0)[16:16 + rn]
            z = z + cw[1:2] * pltpu.roll(ext, 1, axis=0)[16:16 + rn]
            z = z + cw[2:3] * cur
            z = z + cw[3:4] * pltpu.roll(ext, rn + 31, axis=0)[16:16 + rn]
            zs_ref[pl.ds(r0, rn), :] = z
            return carry

        lax.fori_loop(0, nch, conv_body, 0)

        def coeffs(r0, d):
            z = zs_ref[pl.ds(r0, rn), :]
            e = jnp.exp2(jnp.dot(z.astype(BF16), wg_ref[d], preferred_element_type=F32) + bg_ref[d])
            r = 1.0 / (1.0 + e[:, :LANES])
            i = 1.0 / (1.0 + e[:, LANES:])
            a = jnp.exp2(nsp2[d] * r)
            return a, _sqrt_nonneg(1.0 - a * a) * (i * z)

        def gate(r0):
            return _silu(g_ref[pl.ds(r0, rn), :].astype(F32))

        if nch == 1:
            a, b = coeffs(0, 0)
            hf, carry_f = _scan_chunk(a, b, carry_f, False)
            a, b = coeffs(0, 1)
            hr, carry_r = _scan_chunk(a, b, carry_r, True)
            out_ref[...] = ((hf + hr) * gate(0)).astype(out_ref.dtype)
            return carry_f, carry_r

        def step(c, carry, combine):
            cf, cr = carry
            r0 = pl.multiple_of(c * rn, rn)
            r1 = pl.multiple_of((nch - 1 - c) * rn, rn)
            a, b = coeffs(r0, 0)
            hf, cf = _scan_chunk(a, b, cf, False)
            a, b = coeffs(r1, 1)
            hr, cr = _scan_chunk(a, b, cr, True)
            if combine:
                out_ref[pl.ds(r0, rn), :] = ((hf + hb_ref[pl.ds(r0, rn), :]) * gate(r0)).astype(out_ref.dtype)
                out_ref[pl.ds(r1, rn), :] = ((hb_ref[pl.ds(r1, rn), :] + hr) * gate(r1)).astype(out_ref.dtype)
            else:
                hb_ref[pl.ds(r0, rn), :] = hf
                hb_ref[pl.ds(r1, rn), :] = hr
            return cf, cr

        carry = lax.fori_loop(0, nch // 2, functools.partial(step, combine=False), (carry_f, carry_r),
                              unroll=LRU_UNROLL)
        return lax.fori_loop(nch // 2, nch, functools.partial(step, combine=True), carry,
                             unroll=LRU_UNROLL)

    zero = jnp.zeros((1, LANES), F32)
    cf, cr = run_seq(xca_ref, gca_ref, oc_ref, zero, zero)
    run_seq(xa_ref, ga_ref, o_ref, cf, cr)


def _lru_branch(u, uc, conv_w, conv_b, wg, bg, lam):
    b, t, _ = u.shape
    tc = uc.shape[1]
    w = conv_w.shape[1]
    nh = w // LANES
    ga0 = 2 * nh
    return pl.pallas_call(
        _lru_kernel,
        out_shape=(jax.ShapeDtypeStruct((b, t, w), BF16), jax.ShapeDtypeStruct((b, tc, w), BF16)),
        grid=(b, nh),
        in_specs=[
            pl.BlockSpec((None, t, LANES), lambda bi, h: (bi, 0, h)),
            pl.BlockSpec((None, t, LANES), lambda bi, h: (bi, 0, ga0 + h)),
            pl.BlockSpec((None, tc, LANES), lambda bi, h: (bi, 0, h)),
            pl.BlockSpec((None, tc, LANES), lambda bi, h: (bi, 0, ga0 + h)),
            pl.BlockSpec((conv_w.shape[0], LANES), lambda bi, h: (0, h)),
            pl.BlockSpec((1, LANES), lambda bi, h: (0, h)),
            pl.BlockSpec((2, None, LANES, 2 * LANES), lambda bi, h: (0, h, 0, 0)),
            pl.BlockSpec((2, None, 1, 2 * LANES), lambda bi, h: (0, h, 0, 0)),
            pl.BlockSpec((2, None, 1, LANES), lambda bi, h: (0, h, 0, 0)),
        ],
        out_specs=(pl.BlockSpec((None, t, LANES), lambda bi, h: (bi, 0, h)),
                   pl.BlockSpec((None, tc, LANES), lambda bi, h: (bi, 0, h))),
        scratch_shapes=[pltpu.VMEM((t, LANES), F32), pltpu.VMEM((t, LANES), F32)],
        compiler_params=_cparams(2),
        name="rglru",
    )(u, u, uc, uc, conv_w, conv_b, wg, bg, lam)


def _w12_kernel(cs_ref, wf_ref, o_ref):
    gd = wf_ref.shape[0]
    for half in range(gd // LANES):
        wf = wf_ref[:, half * LANES:(half + 1) * LANES]
        o_ref[half, :, :LANES] = jnp.dot(cs_ref[:gd, :], wf, preferred_element_type=F32,
                                         precision=lax.Precision.HIGHEST).astype(o_ref.dtype)
        o_ref[half, :, LANES:] = jnp.dot(cs_ref[gd:, :], wf, preferred_element_type=F32,
                                         precision=lax.Precision.HIGHEST).astype(o_ref.dtype)


def _fourier_weights(fnet_w):
    ne, ng, gd, _ = fnet_w.shape
    nhalf = gd // LANES
    idx = np.arange(gd)
    ang = 2.0 * np.pi * ((idx[:, None] * idx[None, :]) % gd) / gd
    cs = np.concatenate([np.cos(ang), np.sin(ang)], axis=0) / math.sqrt(gd)
    out = pl.pallas_call(
        _w12_kernel,
        out_shape=jax.ShapeDtypeStruct((ne, ng, nhalf, gd, 2 * LANES), BF16),
        grid=(ne, ng),
        in_specs=[pl.BlockSpec((2 * gd, gd), lambda e, g: (0, 0)),
                  pl.BlockSpec((None, None, gd, gd), lambda e, g: (e, g, 0, 0))],
        out_specs=pl.BlockSpec((None, None, nhalf, gd, 2 * LANES), lambda e, g: (e, g, 0, 0, 0)),
        compiler_params=_cparams(2),
        name="fourier_weights",
    )(jnp.asarray(cs, F32), fnet_w)
    return out.reshape(ne, ng * nhalf, gd, 2 * LANES)


def _chanmix_kernel(xb_ref, w_ref, ur_ref, ui_ref):
    ncb, gd, _ = w_ref.shape
    per_group = gd // LANES
    for cb in range(ncb):
        g = cb // per_group
        r = jnp.dot(xb_ref[:, g * gd:(g + 1) * gd], w_ref[cb], preferred_element_type=F32)
        ur_ref[:, cb * LANES:(cb + 1) * LANES] = r[:, :LANES]
        ui_ref[:, cb * LANES:(cb + 1) * LANES] = r[:, LANES:]


def _chanmix(u, w12):
    b, t, _ = u.shape
    ncb, gd, _ = w12.shape
    w = ncb * LANES
    tm = min(ROW_TILE, t)
    out = jax.ShapeDtypeStruct((b, t, w), F32)
    return pl.pallas_call(
        _chanmix_kernel,
        out_shape=(out, out),
        grid=(b, t // tm),
        in_specs=[pl.BlockSpec((None, tm, w), lambda bi, i: (bi, i, 1)),
                  pl.BlockSpec((ncb, gd, 2 * LANES), lambda bi, i: (0, 0, 0))],
        out_specs=(pl.BlockSpec((None, tm, w), lambda bi, i: (bi, i, 0)),
                   pl.BlockSpec((None, tm, w), lambda bi, i: (bi, i, 0))),
        compiler_params=_cparams(2),
        name="fourier_chanmix",
    )(u, w12)


def _dft_a_kernel(xb_ref, w_ref, ea_ref, twr_ref, twi_ref, br_ref, bi_ref, u_s):
    n2, nj, gd = xb_ref.shape
    ncol = w_ref.shape[0]
    xb = xb_ref[...].reshape(n2 * nj, gd)
    for c in range(ncol):
        u = jnp.dot(xb, w_ref[c], preferred_element_type=F32)
        u_s[2 * c] = u[:, :LANES]
        u_s[2 * c + 1] = u[:, LANES:]
    for c in range(ncol):
        ur_s, ui_s = u_s.at[2 * c], u_s.at[2 * c + 1]
        xr = jnp.concatenate([ur_s[pl.ds(j, n2, stride=nj), :] for j in range(nj)], axis=1)
        xi = jnp.concatenate([ui_s[pl.ds(j, n2, stride=nj), :] for j in range(nj)], axis=1)
        rhs = jnp.concatenate([xr, xi], axis=0).astype(BF16)
        res = jnp.dot(ea_ref[...], rhs, preferred_element_type=F32)
        cols = slice(c * LANES, (c + 1) * LANES)
        for j in range(nj):
            pr = res[:n2, j * LANES:(j + 1) * LANES]
            pi = res[n2:, j * LANES:(j + 1) * LANES]
            tr = twr_ref[j]
            ti = twi_ref[j]
            br_ref[j, :, cols] = (pr * tr - pi * ti).astype(br_ref.dtype)
            bi_ref[j, :, cols] = (pr * ti + pi * tr).astype(bi_ref.dtype)


def _dft_b_kernel(br_ref, bi_ref, gb_ref, eb_ref, o_ref, s_ref):
    n1, nj, width = br_ref.shape
    ncol = width // LANES
    for c in range(ncol):
        cols = slice(c * LANES, (c + 1) * LANES)
        sr_ref, si_ref, so_ref = s_ref.at[3 * c], s_ref.at[3 * c + 1], s_ref.at[3 * c + 2]
        sr_ref[...] = br_ref[:, :, cols].astype(F32).reshape(n1 * nj, LANES)
        si_ref[...] = bi_ref[:, :, cols].astype(F32).reshape(n1 * nj, LANES)
        xr = jnp.concatenate([sr_ref[pl.ds(j, n1, stride=nj), :] for j in range(nj)], axis=1)
        xi = jnp.concatenate([si_ref[pl.ds(j, n1, stride=nj), :] for j in range(nj)], axis=1)
        rhs = jnp.concatenate([xr, xi], axis=0).astype(BF16)
        res = jnp.dot(eb_ref[...], rhs, preferred_element_type=F32)
        for j in range(nj):
            so_ref[pl.ds(j, n1, stride=nj), :] = res[:, j * LANES:(j + 1) * LANES]
        fy = so_ref[...].reshape(n1, nj, LANES)
        o_ref[:, :, cols] = (fy * _silu(gb_ref[:, :, cols].astype(F32))).astype(o_ref.dtype)


def _dft_consts(t):
    n2 = FFT_N2
    n1 = t // n2
    i2 = np.arange(n2)
    a2 = 2.0 * np.pi * ((i2[:, None] * i2[None, :]) % n2) / n2
    er, ei = np.cos(a2), np.sin(a2)
    ea = np.block([[er, -ei], [ei, er]])
    i1 = np.arange(n1)
    atw = 2.0 * np.pi * ((i1[:, None] * i2[None, :]) % t) / t
    a1 = 2.0 * np.pi * ((i1[:, None] * i1[None, :]) % n1) / n1
    eb = np.concatenate([np.cos(a1), -np.sin(a1)], axis=1) / math.sqrt(t)
    twr = jnp.broadcast_to(jnp.asarray(np.cos(atw), F32)[:, :, None], (n1, n2, LANES))
    twi = jnp.broadcast_to(jnp.asarray(np.sin(atw), F32)[:, :, None], (n1, n2, LANES))
    return jnp.asarray(ea, F32).astype(BF16), twr, twi, jnp.asarray(eb, F32).astype(BF16)


def _seq_fourier(u, w12, consts):
    b, t, n_in = u.shape
    ncb, gd, _ = w12.shape
    w = ncb * LANES
    n2 = FFT_N2
    n1 = t // n2
    nj = FFT_J
    ea, twr, twi, eb = consts
    u4 = u.reshape(b, n2, n1, n_in)
    xb0 = w // gd
    per_group = gd // LANES
    mid = jax.ShapeDtypeStruct((b, n1, n2, w), BF16)
    br, bi = pl.pallas_call(
        _dft_a_kernel,
        out_shape=(mid, mid),
        grid=(n1 // nj, b, ncb // per_group),
        in_specs=[
            pl.BlockSpec((None, n2, nj, gd), lambda tb, bi_, g: (bi_, 0, tb, xb0 + g)),
            pl.BlockSpec((per_group, gd, 2 * LANES), lambda tb, bi_, g: (g, 0, 0)),
            pl.BlockSpec((2 * n2, 2 * n2), lambda tb, bi_, g: (0, 0)),
            pl.BlockSpec((nj, n2, LANES), lambda tb, bi_, g: (tb, 0, 0)),
            pl.BlockSpec((nj, n2, LANES), lambda tb, bi_, g: (tb, 0, 0)),
        ],
        out_specs=(pl.BlockSpec((None, nj, n2, gd), lambda tb, bi_, g: (bi_, tb, 0, g)),
                   pl.BlockSpec((None, nj, n2, gd), lambda tb, bi_, g: (bi_, tb, 0, g))),
        scratch_shapes=[pltpu.VMEM((2 * per_group, n2 * nj, LANES), F32)],
        compiler_params=_cparams(3),
        name="dft_stage_a",
    )(u4, w12, ea, twr, twi)
    gb0 = (n_in - w) // gd
    out = pl.pallas_call(
        _dft_b_kernel,
        out_shape=jax.ShapeDtypeStruct((b, n1, n2, w), BF16),
        grid=(b, n2 // nj, w // gd),
        in_specs=[
            pl.BlockSpec((None, n1, nj, gd), lambda bi_, kb, g: (bi_, 0, kb, g)),
            pl.BlockSpec((None, n1, nj, gd), lambda bi_, kb, g: (bi_, 0, kb, g)),
            pl.BlockSpec((None, n1, nj, gd), lambda bi_, kb, g: (bi_, 0, kb, gb0 + g)),
            pl.BlockSpec((n1, 2 * n1), lambda bi_, kb, g: (0, 0)),
        ],
        out_specs=pl.BlockSpec((None, n1, nj, gd), lambda bi_, kb, g: (bi_, 0, kb, g)),
        scratch_shapes=[pltpu.VMEM((3 * per_group, n1 * nj, LANES), F32)],
        compiler_params=_cparams(3),
        name="dft_stage_b",
    )(br, bi, u.reshape(b, n1, n2, n_in), eb)
    return out.reshape(b, t, w)


def _dft_dense_kernel(ur_ref, ui_ref, gb_ref, ec_ref, o_ref):
    rhs = jnp.concatenate([ur_ref[...], ui_ref[...]], axis=0).astype(BF16)
    fy = jnp.dot(ec_ref[...], rhs, preferred_element_type=F32)
    o_ref[...] = (fy * _silu(gb_ref[...].astype(F32))).astype(o_ref.dtype)


def _seq_dft_dense(ur, ui, u):
    b, t, w = ur.shape
    gb_blk = u.shape[-1] // w - 1
    it = np.arange(t)
    ang = 2.0 * np.pi * ((it[:, None] * it[None, :]) % t) / t
    ec = jnp.asarray(np.concatenate([np.cos(ang), -np.sin(ang)], axis=1) / math.sqrt(t), F32).astype(BF16)
    return pl.pallas_call(
        _dft_dense_kernel,
        out_shape=jax.ShapeDtypeStruct((b, t, w), BF16),
        grid=(b,),
        in_specs=[pl.BlockSpec((None, t, w), lambda bi: (bi, 0, 0)),
                  pl.BlockSpec((None, t, w), lambda bi: (bi, 0, 0)),
                  pl.BlockSpec((None, t, w), lambda bi: (bi, 0, gb_blk)),
                  pl.BlockSpec((t, 2 * t), lambda bi: (0, 0))],
        out_specs=pl.BlockSpec((None, t, w), lambda bi: (bi, 0, 0)),
        compiler_params=_cparams(1),
        name="dft_dense",
    )(ur, ui, u, ec)


def _ret_state_kernel(lg_ref, k_ref, v_ref, s0_ref, sr_ref, se_ref, s_ref, kd_ref):
    nh, dk, dv = s_ref.shape
    nsub = sr_ref.shape[0]
    cr = k_ref.shape[0] // nsub

    @pl.when(pl.program_id(1) == 0)
    def _init():
        s_ref[...] = s0_ref[...]
        rk = lax.broadcasted_iota(jnp.int32, (cr, dk), 0).astype(F32)
        for h in range(nh):
            kd_ref[h] = jnp.exp(lg_ref[1, h] * rk)

    for sub in range(nsub - 1, -1, -1):
        rows = slice(sub * cr, (sub + 1) * cr)
        for h in range(nh):
            cdec = jnp.exp(jnp.full((1, dv), lg_ref[1, h] * cr, F32))
            st = s_ref[h]
            sr_ref[sub, h] = st.astype(sr_ref.dtype)
            kdec = (k_ref[rows, h * dk:(h + 1) * dk].astype(F32) * kd_ref[h]).astype(BF16)
            s_ref[h] = cdec * st + lax.dot_general(kdec, v_ref[rows, h * dv:(h + 1) * dv],
                                                   (((0,), (0,)), ((), ())), preferred_element_type=F32)

    @pl.when(pl.program_id(1) == pl.num_programs(1) - 1)
    def _fin():
        se_ref[...] = s_ref[...]


def _ret_main_kernel(lg_ref, q_ref, k_ref, v_ref, sr_ref, s0_ref, o_ref, se_ref,
                     s_ref, dm_ref, qf_ref, qr_ref, kd_ref):
    nh, dk, dv = s_ref.shape
    nsub = sr_ref.shape[0]
    cr = q_ref.shape[0] // nsub

    @pl.when(pl.program_id(1) == 0)
    def _init():
        s_ref[...] = s0_ref[...]
        ii = lax.broadcasted_iota(jnp.int32, (cr, cr), 0).astype(F32)
        jj = lax.broadcasted_iota(jnp.int32, (cr, cr), 1).astype(F32)
        diff = jj - ii
        past = diff <= 0
        rk = lax.broadcasted_iota(jnp.int32, (cr, dk), 0).astype(F32)
        for h in range(nh):
            lgf = lg_ref[0, h]
            lgr = lg_ref[1, h]
            dm_ref[h] = jnp.where(past, jnp.exp(lgf * jnp.where(past, -diff, 0.0)),
                                  jnp.exp(lgr * jnp.where(past, 0.0, diff)))
            qf_ref[h] = jnp.exp(lgf * (rk + 1.0))
            qr_ref[h] = jnp.exp(lgr * (cr - rk))
            kd_ref[h] = jnp.exp(lgf * ((cr - 1.0) - rk))

    for sub in range(nsub):
        rows = slice(sub * cr, (sub + 1) * cr)
        for h in range(nh):
            cdec = jnp.exp(jnp.full((1, dv), lg_ref[0, h] * cr, F32))
            qh = q_ref[rows, h * dk:(h + 1) * dk]
            kh = k_ref[rows, h * dk:(h + 1) * dk]
            vh = v_ref[rows, h * dv:(h + 1) * dv]
            s = lax.dot_general(qh, kh, (((1,), (1,)), ((), ())), preferred_element_type=F32)
            st = s_ref[h]
            qf = qh.astype(F32)
            lhs = jnp.concatenate([(s * dm_ref[h]).astype(BF16), (qf * qf_ref[h]).astype(BF16),
                                   (qf * qr_ref[h]).astype(BF16)], axis=1)
            rhs = jnp.concatenate([vh, st.astype(BF16), sr_ref[sub, h]], axis=0)
            o = jnp.dot(lhs, rhs, preferred_element_type=F32)
            kdec = (kh.astype(F32) * kd_ref[h]).astype(BF16)
            s_ref[h] = cdec * st + lax.dot_general(kdec, vh, (((0,), (0,)), ((), ())),
                                                   preferred_element_type=F32)
            mu = jnp.mean(o, axis=-1, keepdims=True)
            oc = o - mu
            var = jnp.mean(oc * oc, axis=-1, keepdims=True)
            o_ref[rows, h * dv:(h + 1) * dv] = (oc * lax.rsqrt(var + EPS)).astype(o_ref.dtype)

    @pl.when(pl.program_id(1) == pl.num_programs(1) - 1)
    def _fin():
        se_ref[...] = s_ref[...]


def _ret_seq(u, lg, s0_f, s0_r):
    b, t, _ = u.shape
    _, nh, dk, dv = s0_f.shape
    cr = min(RET_CHUNK, t)
    nc = t // cr
    qk = nh * dk
    vw = nh * dv
    v_blk = (2 * qk) // vw
    state = jax.ShapeDtypeStruct((b, nh, dk, dv), F32)
    state_spec = pl.BlockSpec((None, nh, dk, dv), lambda bi, i: (bi, 0, 0, 0))
    smem = pl.BlockSpec(memory_space=pltpu.SMEM)
    nsub = math.gcd(nc, STATE_CHUNKS)
    nmain = math.gcd(nc, MAIN_CHUNKS)
    ns = nc // nsub
    sr, se_r = pl.pallas_call(
        _ret_state_kernel,
        out_shape=(jax.ShapeDtypeStruct((b, nc, nh, dk, dv), BF16), state),
        grid=(b, ns),
        in_specs=[smem,
                  pl.BlockSpec((None, nsub * cr, qk), lambda bi, i: (bi, ns - 1 - i, 1)),
                  pl.BlockSpec((None, nsub * cr, vw), lambda bi, i: (bi, ns - 1 - i, v_blk)),
                  state_spec],
        out_specs=(pl.BlockSpec((None, nsub, nh, dk, dv), lambda bi, i: (bi, ns - 1 - i, 0, 0, 0)),
                   state_spec),
        scratch_shapes=[pltpu.VMEM((nh, dk, dv), F32), pltpu.VMEM((nh, cr, dk), F32)],
        compiler_params=_cparams(2),
        name="retention_state",
    )(lg, u, u, s0_r)
    o, se_f = pl.pallas_call(
        _ret_main_kernel,
        out_shape=(jax.ShapeDtypeStruct((b, t, vw), BF16), state),
        grid=(b, nc // nmain),
        in_specs=[smem,
                  pl.BlockSpec((None, nmain * cr, qk), lambda bi, i: (bi, i, 0)),
                  pl.BlockSpec((None, nmain * cr, qk), lambda bi, i: (bi, i, 1)),
                  pl.BlockSpec((None, nmain * cr, vw), lambda bi, i: (bi, i, v_blk)),
                  pl.BlockSpec((None, nmain, nh, dk, dv), lambda bi, i: (bi, i, 0, 0, 0)),
                  state_spec],
        out_specs=(pl.BlockSpec((None, nmain * cr, vw), lambda bi, i: (bi, i, 0)), state_spec),
        scratch_shapes=[
            pltpu.VMEM((nh, dk, dv), F32),
            pltpu.VMEM((nh, cr, cr), F32),
            pltpu.VMEM((nh, cr, dk), F32),
            pltpu.VMEM((nh, cr, dk), F32),
            pltpu.VMEM((nh, cr, dk), F32),
        ],
        compiler_params=_cparams(2),
        name="retention_main",
    )(lg, u, u, u, sr, s0_f)
    return o, se_f, se_r


def _retention(u, uc, lg):
    b = u.shape[0]
    nh = RET_HEADS
    dk = u.shape[-1] // (6 * nh)
    zero = jnp.zeros((b, nh, dk, 2 * dk), F32)
    mc, s_f, s_r = _ret_seq(uc, lg, zero, zero)
    m, _, _ = _ret_seq(u, lg, s_f, s_r)
    return m, mc


def _rope_tables(t):
    nf = LANES // 2
    inv = ROPE_THETA ** (-jnp.arange(nf, dtype=F32) / nf)
    ar = jnp.arange(t // GRID_W, dtype=F32)[:, None] * inv
    ac = jnp.arange(GRID_W, dtype=F32)[:, None] * inv
    dup = lambda v: jnp.concatenate([v, v], axis=1)
    sgn = lambda v: jnp.concatenate([-v, v], axis=1)
    return dup(jnp.cos(ar)), sgn(jnp.sin(ar)), dup(jnp.cos(ac)), sgn(jnp.sin(ac))


def kernel(x, c, ctx, c_ctx, mod_w, mod_b, pre_g, post_g, mix_w_in, mix_w_out, conv_w, conv_b,
           lru_wa, lru_ba, lru_wx, lru_bx, lru_lam, fnet_w, ret_w_in, ret_w_out, ret_log_gamma):
    b, t, d = x.shape
    tc = ctx.shape[1]
    depth = mod_w.shape[0]
    assert t % ROW_TILE == 0 and t % (FFT_N2 * FFT_J) == 0 and (t // FFT_N2) % FFT_J == 0
    assert tc % GRID_W == 0 and tc % 16 == 0 and tc <= ROW_TILE and b + 1 <= 8

    cond8 = jnp.zeros((8, d), F32).at[:b].set(c).at[b].set(c_ctx)
    mods = _ada_mod(cond8, mod_w, mod_b)

    w12 = _fourier_weights(fnet_w)
    dft_consts = _dft_consts(t)
    qk = ret_w_in.shape[-1] // 6
    rope_l = _rope_tables(t) + (2 * qk,)
    ones = lambda n: jnp.ones((n, LANES), F32)
    zeros = lambda n: jnp.zeros((n, LANES), F32)
    rope_c = (ones(tc // GRID_W), zeros(tc // GRID_W), ones(GRID_W), zeros(GRID_W), 2 * qk)

    nh_l = LRU_HEADS
    hd = lru_wa.shape[-1]
    wg_all = (jnp.concatenate([lru_wa, lru_wx], axis=-1) * (-LOG2E)).astype(BF16)
    bg_all = jnp.concatenate([lru_ba.reshape(-1, 2, nh_l, 1, hd),
                              lru_bx.reshape(-1, 2, nh_l, 1, hd)], axis=-1) * (-LOG2E)
    lam_all = lru_lam.reshape(-1, 2, nh_l, 1, hd)

    kscale = jnp.concatenate([jnp.ones((qk,), F32), jnp.full((qk,), (qk // RET_HEADS) ** -0.5, F32),
                              jnp.ones((4 * qk,), F32)])

    def w_in_of(layer):
        if layer % 2 == 0:
            return mix_w_in[layer // 2].astype(BF16)
        return (ret_w_in[layer // 2] * kscale).astype(BF16)

    def mod_of(layer, latent):
        m_l = mods[layer]
        parts = (m_l[:, :d], m_l[:, d:2 * d], m_l[:, 2 * d:])
        if latent:
            return tuple(v[:b].reshape(b, 1, d) for v in parts)
        return tuple(jnp.broadcast_to(v[b].reshape(1, 1, d), (b, 1, d)) for v in parts)

    def inp_of(layer, latent):
        shift, scale, _ = mod_of(layer, latent)
        rope = None if layer % 2 == 0 else (rope_l if latent else rope_c)
        return (pre_g[layer].reshape(1, d), scale, shift, w_in_of(layer), rope)

    xc = ctx
    _, u = _proj(x, inp=inp_of(0, True))
    _, uc = _proj(xc, inp=inp_of(0, False))
    for layer in range(depth):
        last = layer == depth - 1
        post = post_g[layer].reshape(1, d)
        gt_l = mod_of(layer, True)[2]
        gt_c = mod_of(layer, False)[2]
        if layer % 2 == 0:
            e = layer // 2
            w_out = mix_w_out[e].astype(BF16)
            ma, mca = _lru_branch(u, uc, conv_w[e], conv_b[e].reshape(1, -1),
                                  wg_all[e], bg_all[e], lam_all[e])
            mb = _seq_fourier(u, w12[e], dft_consts)
            out_l = ("even", ma, mb, w_out, post, gt_l)
            if not last:
                urc, uic = _chanmix(uc, w12[e])
                out_c = ("even", mca, _seq_dft_dense(urc, uic, uc), w_out, post, gt_c)
        else:
            j = layer // 2
            w_out = ret_w_out[j].astype(BF16)
            m, mc = _retention(u, uc, ret_log_gamma[j])
            out_l = ("odd", m, u, w_out, post, gt_l)
            out_c = ("odd", mc, uc, w_out, post, gt_c)
        if last:
            x, _ = _proj(x, out=out_l)
        else:
            x, u = _proj(x, out=out_l, inp=inp_of(layer + 1, True))
            xc, uc = _proj(xc, out=out_c, inp=inp_of(layer + 1, False))
    return x
```
